```python
import jax, jax.numpy as jnp
from jax import lax
import numpy as np

D_MODEL = 2048
BATCH = 16
SEQ = 2048
DEPTH = 4

CHUNK = 64
Q_BLOCK = 128
HEAD_DIM = 128
H_SB = 6
H_DIFF = 5
H_FOX = 5
DIFF_DIM = HEAD_DIM // 2
D_SB = H_SB * HEAD_DIM
D_DIFF = H_DIFF * HEAD_DIM
D_FOX = H_FOX * HEAD_DIM
N_BRANCH = 3
D_IN = 3 * D_SB + 3 * D_DIFF + 3 * D_FOX + H_FOX
D_FF = -(-8 * D_MODEL // (3 * 256)) * 256
EPS = 1e-6
ALIBI_MAX = 8.0

kernel_name = "hybrid_stickbreak_diff_fox_trunk"


def _rms(x, g):
    xf = x.astype(jnp.float32)
    y = xf * lax.rsqrt(jnp.mean(xf * xf, axis=-1, keepdims=True) + EPS)
    return (y * g.astype(jnp.float32)).astype(x.dtype)


def _heads(t, n, d):
    b, s, _ = t.shape
    return t.reshape(b, s, n, d).transpose(0, 2, 1, 3)


def _merge(o):
    b, h, s, d = o.shape
    return o.transpose(0, 2, 1, 3).reshape(b, s, h * d)


def _stick_breaking(q, k, v):
    seq, d = q.shape[2], q.shape[3]
    scale = d ** -0.5
    outs = []
    for i in range(seq // Q_BLOCK):
        q0, q1 = i * Q_BLOCK, (i + 1) * Q_BLOCK
        z = jnp.einsum('bhqd,bhkd->bhqk', q[:, :, q0:q1], k[:, :, :q1]).astype(jnp.float32) * scale
        tq = jnp.arange(q0, q1)[:, None]
        sk = jnp.arange(q1)[None, :]
        before = sk < tq
        log_not = jnp.where(before, -jax.nn.softplus(z), 0.0)
        tail = lax.cumsum(log_not, axis=3, reverse=True) - log_not
        w = jnp.where(before, jnp.exp(jax.nn.log_sigmoid(z) + tail), 0.0)
        outs.append(jnp.einsum('bhqk,bhkd->bhqd', w.astype(v.dtype), v[:, :, :q1]))
    return jnp.concatenate(outs, axis=2)


def _diff_attention(q1, q2, k1, k2, v, lam, slopes):
    seq = q1.shape[2]
    scale = DIFF_DIM ** -0.5
    outs = []
    for i in range(seq // Q_BLOCK):
        q0, qe = i * Q_BLOCK, (i + 1) * Q_BLOCK
        tq = jnp.arange(q0, qe)[:, None]
        sk = jnp.arange(qe)[None, :]
        visible = (sk // CHUNK) <= (tq // CHUNK)
        bias = -slopes[:, None, None] * jnp.abs(tq - sk).astype(jnp.float32)

        def probs(qa, ka):
            s = jnp.einsum('bhqd,bhkd->bhqk', qa[:, :, q0:qe], ka[:, :, :qe]).astype(jnp.float32) * scale + bias
            return jax.nn.softmax(jnp.where(visible, s, -jnp.inf), axis=-1)

        a = probs(q1, k1) - lam * probs(q2, k2)
        outs.append(jnp.einsum('bhqk,bhkd->bhqd', a.astype(v.dtype), v[:, :, :qe]))
    return jnp.concatenate(outs, axis=2)


def _forgetting_attention(q, k, v, log_f):
    seq, d = q.shape[2], q.shape[3]
    scale = d ** -0.5
    cum = jnp.cumsum(log_f, axis=-1)
    outs = []
    for i in range(seq // Q_BLOCK):
        q0, q1 = i * Q_BLOCK, (i + 1) * Q_BLOCK
        tq = jnp.arange(q0, q1)[:, None]
        sk = jnp.arange(q1)[None, :]
        s = jnp.einsum('bhqd,bhkd->bhqk', q[:, :, q0:q1], k[:, :, :q1]).astype(jnp.float32) * scale
        s = s + cum[:, :, q0:q1, None] - cum[:, :, None, :q1]
        p = jax.nn.softmax(jnp.where(sk <= tq, s, -jnp.inf), axis=-1)
        outs.append(jnp.einsum('bhqk,bhkd->bhqd', p.astype(v.dtype), v[:, :, :q1]))
    return jnp.concatenate(outs, axis=2)


def setup_inputs(seed: int = 0) -> dict:
    key = jax.random.key(seed)
    ks = jax.random.split(key, 24)
    L, D = DEPTH, D_MODEL

    def w(k, shape, fan_in):
        return jax.random.normal(k, shape, jnp.float32) * fan_in ** -0.5

    def gain(k, shape):
        return 1.0 + 0.02 * jax.random.normal(k, shape, jnp.float32)

    return {
        "x": jax.random.normal(ks[0], (BATCH, SEQ, D), jnp.float32),
        "norm_mix": gain(ks[1], (L, D)),
        "w_in": w(ks[2], (L, D, D_IN), D),
        "b_forget": 3.0 + 0.5 * jax.random.normal(ks[3], (L, H_FOX), jnp.float32),
        "q_norm_diff": gain(ks[4], (L, DIFF_DIM)),
        "k_norm_diff": gain(ks[5], (L, DIFF_DIM)),
        "lambda_q1": 0.1 * jax.random.normal(ks[6], (L, DIFF_DIM), jnp.float32),
        "lambda_k1": 0.1 * jax.random.normal(ks[7], (L, DIFF_DIM), jnp.float32),
        "lambda_q2": 0.1 * jax.random.normal(ks[8], (L, DIFF_DIM), jnp.float32),
        "lambda_k2": 0.1 * jax.random.normal(ks[9], (L, DIFF_DIM), jnp.float32),
        "sub_norm_diff": gain(ks[10], (L, HEAD_DIM)),
        "q_norm_fox": gain(ks[11], (L, HEAD_DIM)),
        "k_norm_fox": gain(ks[12], (L, HEAD_DIM)),
        "w_branch_sb": w(ks[13], (L, D_SB, D), D_SB),
        "w_branch_diff": w(ks[14], (L, D_DIFF, D), D_DIFF),
        "w_branch_fox": w(ks[15], (L, D_FOX, D), D_FOX),
        "w_gate": w(ks[16], (L, D, N_BRANCH * D), D),
        "w_out": w(ks[17], (L, D, D), D),
        "norm_ffn": gain(ks[18], (L, D)),
        "w_ff_gate": w(ks[19], (L, D, D_FF), D),
        "w_ff_up": w(ks[20], (L, D, D_FF), D),
        "w_ff_down": w(ks[21], (L, D_FF, D), D_FF),
    }


def reference(x, norm_mix, w_in, b_forget, q_norm_diff, k_norm_diff, lambda_q1, lambda_k1,
              lambda_q2, lambda_k2, sub_norm_diff, q_norm_fox, k_norm_fox, w_branch_sb,
              w_branch_diff, w_branch_fox, w_gate, w_out, norm_ffn, w_ff_gate, w_ff_up, w_ff_down):
    b, s, d = x.shape
    slopes = 2.0 ** (-ALIBI_MAX * jnp.arange(1, H_DIFF + 1, dtype=jnp.float32) / H_DIFF)
    o1 = 3 * D_SB
    o2 = o1 + 3 * D_DIFF
    o3 = o2 + 3 * D_FOX
    for l in range(DEPTH):
        lambda_init = 0.8 - 0.6 * float(np.exp(-0.3 * l))
        xn = _rms(x, norm_mix[l])
        h = xn @ w_in[l]

        q_a, k_a, v_a = jnp.split(h[..., :o1], 3, axis=-1)
        o_a = _stick_breaking(_heads(q_a, H_SB, HEAD_DIM), _heads(k_a, H_SB, HEAD_DIM),
                              _heads(v_a, H_SB, HEAD_DIM))

        q_b, k_b, v_b = jnp.split(h[..., o1:o2], 3, axis=-1)
        qd = _rms(_heads(q_b, 2 * H_DIFF, DIFF_DIM), q_norm_diff[l]).reshape(b, H_DIFF, 2, s, DIFF_DIM)
        kd = _rms(_heads(k_b, 2 * H_DIFF, DIFF_DIM), k_norm_diff[l]).reshape(b, H_DIFF, 2, s, DIFF_DIM)
        lam = (jnp.exp(jnp.sum(lambda_q1[l].astype(jnp.float32) * lambda_k1[l].astype(jnp.float32)))
               - jnp.exp(jnp.sum(lambda_q2[l].astype(jnp.float32) * lambda_k2[l].astype(jnp.float32)))
               + lambda_init)
        o_b = _diff_attention(qd[:, :, 0], qd[:, :, 1], kd[:, :, 0], kd[:, :, 1],
                              _heads(v_b, H_DIFF, HEAD_DIM), lam, slopes)
        o_b = _rms(o_b, sub_norm_diff[l]) * (1.0 - lambda_init)

        q_c, k_c, v_c = jnp.split(h[..., o2:o3], 3, axis=-1)
        log_f = jax.nn.log_sigmoid(h[..., o3:].astype(jnp.float32)
                                   + b_forget[l].astype(jnp.float32)).transpose(0, 2, 1)
        o_c = _forgetting_attention(_rms(_heads(q_c, H_FOX, HEAD_DIM), q_norm_fox[l]),
                                    _rms(_heads(k_c, H_FOX, HEAD_DIM), k_norm_fox[l]),
                                    _heads(v_c, H_FOX, HEAD_DIM), log_f)

        gates = jax.nn.sigmoid((xn @ w_gate[l]).astype(jnp.float32)).astype(x.dtype).reshape(b, s, N_BRANCH, d)
        y = (gates[:, :, 0] * (_merge(o_a) @ w_branch_sb[l])
             + gates[:, :, 1] * (_merge(o_b) @ w_branch_diff[l])
             + gates[:, :, 2] * (_merge(o_c) @ w_branch_fox[l]))
        x = x + y @ w_out[l]

        xf = _rms(x, norm_ffn[l])
        x = x + (jax.nn.silu(xf @ w_ff_gate[l]) * (xf @ w_ff_up[l])) @ w_ff_down[l]
    return x
```

```python
import functools

import jax
import jax.numpy as jnp
import numpy as np
from jax import lax
from jax.experimental import pallas as pl
from jax.experimental.pallas import tpu as pltpu

F32 = jnp.float32
BF16 = jnp.bfloat16

HEAD_DIM = 128
DIFF_DIM = HEAD_DIM // 2
H_SB = 6
H_DIFF = 5
H_FOX = 5
CHUNK = 64
EPS = 1e-6
ALIBI_MAX = 8.0

LANES = 128
SUBLANES = 8
VMEM_LIMIT_BYTES = 56 * 1024 * 1024

TM_PROJ = 1024
TN_PROJ = 512
TM_FFN = 512
TF_FFN = 512
TQ = 256
TK = 256
NEG = -1e30


def _cparams(*sem):
    return pltpu.CompilerParams(dimension_semantics=sem, vmem_limit_bytes=VMEM_LIMIT_BYTES)


def _rms_rows(x, g):
    ms = jnp.mean(x * x, axis=-1, keepdims=True)
    return x * lax.rsqrt(ms + EPS) * g


def _softplus(z):
    return jnp.maximum(z, 0.0) + jnp.log(1.0 + jnp.exp(-jnp.abs(z)))


def _sigmoid(z):
    return 1.0 / (1.0 + jnp.exp(-z))


def _dot_nt(a, b):
    return lax.dot_general(a, b, (((1,), (1,)), ((), ())), preferred_element_type=F32)


def _dot(a, b):
    return jnp.dot(a, b, preferred_element_type=F32)


def _norm_proj_kernel(x_ref, g_ref, w_ref, o_ref, xn_ref, *, act):
    @pl.when(pl.program_id(1) == 0)
    def _():
        xn_ref[...] = _rms_rows(x_ref[...], g_ref[...]).astype(BF16)

    acc = _dot(xn_ref[...], w_ref[...])
    if act == "sigmoid":
        acc = _sigmoid(acc)
    o_ref[...] = acc.astype(o_ref.dtype)


def _norm_proj(x2, g, w, *, act, out_dtype, tn):
    m, d = x2.shape
    n = w.shape[1]
    tm = min(TM_PROJ, m)
    return pl.pallas_call(
        functools.partial(_norm_proj_kernel, act=act),
        grid=(m // tm, n // tn),
        in_specs=[
            pl.BlockSpec((tm, d), lambda i, j: (i, 0)),
            pl.BlockSpec((1, d), lambda i, j: (0, 0)),
            pl.BlockSpec((d, tn), lambda i, j: (0, j)),
        ],
        out_specs=pl.BlockSpec((tm, tn), lambda i, j: (i, j)),
        out_shape=jax.ShapeDtypeStruct((m, n), out_dtype),
        scratch_shapes=[pltpu.VMEM((tm, d), BF16)],
        compiler_params=_cparams("parallel", "arbitrary"),
        name="norm_proj_" + str(act),
    )(x2, g, w)


def _forget_cum_kernel(hf_ref, b_ref, o_ref, *, blk):
    s = hf_ref.shape[1]
    z = hf_ref[0] + b_ref[...]
    lf = jnp.minimum(z, 0.0) - jnp.log(1.0 + jnp.exp(-jnp.abs(z)))
    lft = lf.T[:SUBLANES]
    row = lax.broadcasted_iota(jnp.int32, (blk, blk), 0)
    col = lax.broadcasted_iota(jnp.int32, (blk, blk), 1)
    upper = (row <= col).astype(BF16)
    carry = jnp.zeros((SUBLANES, 1), F32)
    for i in range(s // blk):
        seg = lft[:, i * blk:(i + 1) * blk]
        p0 = seg.astype(BF16)
        r0 = seg - p0.astype(F32)
        p1 = r0.astype(BF16)
        p2 = (r0 - p1.astype(F32)).astype(BF16)
        cs = _dot(p0, upper) + _dot(p1, upper) + _dot(p2, upper) + carry
        o_ref[0, :, i * blk:(i + 1) * blk] = cs
        carry = cs[:, blk - 1:blk]


def _forget_cum(hf, b_pad):
    b, s, _ = hf.shape
    blk = min(256, s)
    return pl.pallas_call(
        functools.partial(_forget_cum_kernel, blk=blk),
        grid=(b,),
        in_specs=[
            pl.BlockSpec((1, s, LANES), lambda i: (i, 0, 0)),
            pl.BlockSpec((1, LANES), lambda i: (0, 0)),
        ],
        out_specs=pl.BlockSpec((1, SUBLANES, s), lambda i: (i, 0, 0)),
        out_shape=jax.ShapeDtypeStruct((b, SUBLANES, s), F32),
        compiler_params=_cparams("parallel"),
        name="forget_cum",
    )(hf, b_pad)


def _sb_kernel(q_ref, k_ref, v_ref, o_ref, *, tq):
    s = q_ref.shape[1]
    scale = HEAD_DIM ** -0.5
    row = lax.broadcasted_iota(jnp.int32, (tq, tq), 0)
    col = lax.broadcasted_iota(jnp.int32, (tq, tq), 1)
    tri = (row >= col).astype(BF16)
    before = col < row

    def block(q, k0, carry, acc, diag):
        k = k_ref[0, pl.ds(k0, tq), :]
        v = v_ref[0, pl.ds(k0, tq), :]
        z = _dot_nt(q, k) * scale
        lnot = -_softplus(z)
        if diag:
            lnot = jnp.where(before, lnot, 0.0)
        hi = lnot.astype(BF16)
        lo = (lnot - hi.astype(F32)).astype(BF16)
        suf = _dot(hi, tri) + _dot(lo, tri)
        arg = z + suf + carry
        if diag:
            arg = jnp.where(before, arg, NEG)
        w = jnp.exp(arg)
        acc = acc + _dot(w.astype(BF16), v)
        return carry + suf[:, 0:1], acc

    def q_body(i, _):
        q0 = pl.multiple_of(i * tq, tq)
        q = q_ref[0, pl.ds(q0, tq), :]
        carry, acc = block(q, q0, jnp.zeros((tq, 1), F32), jnp.zeros((tq, HEAD_DIM), F32), True)

        def k_body(j, st):
            k0 = pl.multiple_of((i - 1 - j) * tq, tq)
            return block(q, k0, st[0], st[1], False)

        carry, acc = lax.fori_loop(0, i, k_body, (carry, acc))
        o_ref[0, pl.ds(q0, tq), :] = acc.astype(o_ref.dtype)
        return 0

    lax.fori_loop(0, s // tq, q_body, 0)


def _stick_breaking(h3, b, s):
    tq = min(TQ, s)
    blk = lambda off: pl.BlockSpec((1, s, HEAD_DIM), lambda bi, hi: (bi, 0, off + hi))
    return pl.pallas_call(
        functools.partial(_sb_kernel, tq=tq),
        grid=(b, H_SB),
        in_specs=[blk(0), blk(H_SB), blk(2 * H_SB)],
        out_specs=pl.BlockSpec((1, s, HEAD_DIM), lambda bi, hi: (bi, 0, hi)),
        out_shape=jax.ShapeDtypeStruct((b, s, H_SB * HEAD_DIM), BF16),
        compiler_params=_cparams("parallel", "parallel"),
        name="stick_breaking",
    )(h3, h3, h3)


def _halves_rms(x, g2, lo_lane):
    x2 = x * x
    s_lo = jnp.sum(jnp.where(lo_lane, x2, 0.0), axis=-1, keepdims=True)
    s_hi = jnp.sum(jnp.where(lo_lane, 0.0, x2), axis=-1, keepdims=True)
    inv = jnp.where(lo_lane, lax.rsqrt(s_lo / DIFF_DIM + EPS), lax.rsqrt(s_hi / DIFF_DIM + EPS))
    return x * inv * g2


def _diff_kernel(q_ref, k_ref, v_ref, gq_ref, gk_ref, gs_ref, lam_ref, slope_ref, o_ref,
                 kn_ref, *, tq, lambda_init):
    s = q_ref.shape[1]
    scale = DIFF_DIM ** -0.5
    lo_lane = lax.broadcasted_iota(jnp.int32, (1, HEAD_DIM), 1) < DIFF_DIM
    row = lax.broadcasted_iota(jnp.int32, (tq, tq), 0)
    col = lax.broadcasted_iota(jnp.int32, (tq, tq), 1)
    visible = (col // CHUNK) <= (row // CHUNK)
    slope = slope_ref[0][:, 0:1]
    rel = (col - row).astype(F32) * slope
    lp = lam_ref[...]
    lam = (jnp.exp(jnp.sum(lp[0:1] * lp[1:2], axis=-1, keepdims=True))
           - jnp.exp(jnp.sum(lp[2:3] * lp[3:4], axis=-1, keepdims=True)) + lambda_init)

    kn_ref[...] = _halves_rms(k_ref[0].astype(F32), gk_ref[...], lo_lane).astype(BF16)

    def online(sc, st, v):
        m, l, acc = st
        m_new = jnp.maximum(m, jnp.max(sc, axis=-1, keepdims=True))
        alpha = jnp.exp(m - m_new)
        p = jnp.exp(sc - m_new)
        l = alpha * l + jnp.sum(p, axis=-1, keepdims=True)
        acc = alpha * acc + _dot(p.astype(BF16), v)
        return m_new, l, acc

    def block(q1, q2, k0, bias, st, diag):
        k = kn_ref[pl.ds(k0, tq), :]
        v = v_ref[0, pl.ds(k0, tq), :]
        s1 = _dot_nt(q1, k) * scale + bias
        s2 = _dot_nt(q2, k) * scale + bias
        if diag:
            s1 = jnp.where(visible, s1, NEG)
            s2 = jnp.where(visible, s2, NEG)
        return online(s1, st[0], v), online(s2, st[1], v)

    def q_body(i, _):
        q0 = pl.multiple_of(i * tq, tq)
        qn = _halves_rms(q_ref[0, pl.ds(q0, tq), :].astype(F32), gq_ref[...], lo_lane)
        q1 = jnp.where(lo_lane, qn, 0.0).astype(BF16)
        q2 = jnp.where(lo_lane, 0.0, qn).astype(BF16)
        init = (jnp.full((tq, 1), NEG, F32), jnp.zeros((tq, 1), F32), jnp.zeros((tq, HEAD_DIM), F32))
        st = block(q1, q2, q0, -jnp.abs(rel), (init, init), True)

        def k_body(j, st):
            k0 = pl.multiple_of(j * tq, tq)
            bias = rel + (k0 - q0).astype(F32) * slope
            return block(q1, q2, k0, bias, st, False)

        (_, l1, a1), (_, l2, a2) = lax.fori_loop(0, i, k_body, st)
        o = a1 / l1 - lam * (a2 / l2)
        o = _rms_rows(o, gs_ref[...]) * (1.0 - lambda_init)
        o_ref[0, pl.ds(q0, tq), :] = o.astype(o_ref.dtype)
        return 0

    lax.fori_loop(0, s // tq, q_body, 0)


def _diff_attention(h3, b, s, gq2, gk2, gs, lam_params, slopes, lambda_init):
    tq = min(TQ, s)
    off = 3 * H_SB
    blk = lambda o: pl.BlockSpec((1, s, HEAD_DIM), lambda bi, hi: (bi, 0, o + hi))
    vec = pl.BlockSpec((1, HEAD_DIM), lambda bi, hi: (0, 0))
    return pl.pallas_call(
        functools.partial(_diff_kernel, tq=tq, lambda_init=lambda_init),
        grid=(b, H_DIFF),
        in_specs=[blk(off), blk(off + H_DIFF), blk(off + 2 * H_DIFF), vec, vec, vec,
                  pl.BlockSpec((4, DIFF_DIM), lambda bi, hi: (0, 0)),
                  pl.BlockSpec((1, 1, LANES), lambda bi, hi: (hi, 0, 0))],
        out_specs=pl.BlockSpec((1, s, HEAD_DIM), lambda bi, hi: (bi, 0, hi)),
        out_shape=jax.ShapeDtypeStruct((b, s, H_DIFF * HEAD_DIM), BF16),
        scratch_shapes=[pltpu.VMEM((s, HEAD_DIM), BF16)],
        compiler_params=_cparams("parallel", "parallel"),
        name="diff_attention",
    )(h3, h3, h3, gq2, gk2, gs, lam_params, slopes)


def _fox_kernel(q_ref, k_ref, v_ref, gq_ref, gk_ref, cum_ref, o_ref, kn_ref, *, tq):
    s = q_ref.shape[1]
    scale = HEAD_DIM ** -0.5
    row = lax.broadcasted_iota(jnp.int32, (tq, tq), 0)
    col = lax.broadcasted_iota(jnp.int32, (tq, tq), 1)
    causal = col <= row

    kn_ref[...] = _rms_rows(k_ref[0].astype(F32), gk_ref[...]).astype(BF16)

    def block(q, k0, c0, st, diag):
        m, l, acc = st
        k = kn_ref[pl.ds(k0, tq), :]
        v = v_ref[0, pl.ds(k0, tq), :]
        sc = _dot_nt(q, k) * scale + (c0 - cum_ref[0, 0, :, pl.ds(k0, tq)])
        if diag:
            sc = jnp.where(causal, sc, NEG)
        m_new = jnp.maximum(m, jnp.max(sc, axis=-1, keepdims=True))
        alpha = jnp.exp(m - m_new)
        p = jnp.exp(sc - m_new)
        l = alpha * l + jnp.sum(p, axis=-1, keepdims=True)
        acc = alpha * acc + _dot(p.astype(BF16), v)
        return m_new, l, acc

    def q_body(i, _):
        q0 = pl.multiple_of(i * tq, tq)
        q = _rms_rows(q_ref[0, pl.ds(q0, tq), :].astype(F32), gq_ref[...]).astype(BF16)
        c0 = cum_ref[0, 0, :, pl.ds(q0, LANES)][:, 0:1]
        init = (jnp.full((tq, 1), NEG, F32), jnp.zeros((tq, 1), F32), jnp.zeros((tq, HEAD_DIM), F32))
        st = block(q, q0, c0, init, True)

        def k_body(j, st):
            return block(q, pl.multiple_of(j * tq, tq), c0, st, False)

        _, l, acc = lax.fori_loop(0, i, k_body, st)
        o_ref[0, pl.ds(q0, tq), :] = (acc / l).astype(o_ref.dtype)
        return 0

    lax.fori_loop(0, s // tq, q_body, 0)


def _fox_attention(h3, b, s, gq, gk, cum4):
    tq = min(TQ, s)
    off = 3 * H_SB + 3 * H_DIFF
    blk = lambda o: pl.BlockSpec((1, s, HEAD_DIM), lambda bi, hi: (bi, 0, o + hi))
    vec = pl.BlockSpec((1, HEAD_DIM), lambda bi, hi: (0, 0))
    return pl.pallas_call(
        functools.partial(_fox_kernel, tq=tq),
        grid=(b, H_FOX),
        in_specs=[blk(off), blk(off + H_FOX), blk(off + 2 * H_FOX), vec, vec,
                  pl.BlockSpec((1, 1, 1, s), lambda bi, hi: (bi, hi, 0, 0))],
        out_specs=pl.BlockSpec((1, s, HEAD_DIM), lambda bi, hi: (bi, 0, hi)),
        out_shape=jax.ShapeDtypeStruct((b, s, H_FOX * HEAD_DIM), BF16),
        scratch_shapes=[pltpu.VMEM((s, HEAD_DIM), BF16)],
        compiler_params=_cparams("parallel", "parallel"),
        name="fox_attention",
    )(h3, h3, h3, gq, gk, cum4)


def _merge_kernel(oa_ref, ob_ref, oc_ref, ga_ref, gb_ref, gc_ref, wa_ref, wb_ref, wc_ref, y_ref):
    y = ga_ref[...].astype(F32) * _dot(oa_ref[...], wa_ref[...])
    y += gb_ref[...].astype(F32) * _dot(ob_ref[...], wb_ref[...])
    y += gc_ref[...].astype(F32) * _dot(oc_ref[...], wc_ref[...])
    y_ref[...] = y.astype(y_ref.dtype)


def _gated_merge(oa, ob, oc, gates, wa, wb, wc):
    m = oa.shape[0]
    d = wa.shape[1]
    tm = min(TM_PROJ, m)
    tn = min(TN_PROJ, d)
    nb = d // tn
    o_spec = lambda a: pl.BlockSpec((tm, a.shape[1]), lambda i, j: (i, 0))
    g_spec = lambda br: pl.BlockSpec((tm, tn), lambda i, j: (i, br * nb + j))
    w_spec = lambda w: pl.BlockSpec((w.shape[0], tn), lambda i, j: (0, j))
    return pl.pallas_call(
        _merge_kernel,
        grid=(m // tm, nb),
        in_specs=[o_spec(oa), o_spec(ob), o_spec(oc), g_spec(0), g_spec(1), g_spec(2),
                  w_spec(wa), w_spec(wb), w_spec(wc)],
        out_specs=pl.BlockSpec((tm, tn), lambda i, j: (i, j)),
        out_shape=jax.ShapeDtypeStruct((m, d), BF16),
        compiler_params=_cparams("parallel", "parallel"),
        name="gated_merge",
    )(oa, ob, oc, gates, gates, gates, wa, wb, wc)


def _resid_proj_kernel(x_ref, y_ref, w_ref, o_ref):
    o_ref[...] = x_ref[...] + _dot(y_ref[...], w_ref[...])


def _resid_proj(x2, y, w):
    m, d = x2.shape
    tm = min(TM_PROJ, m)
    tn = min(TN_PROJ, d)
    return pl.pallas_call(
        _resid_proj_kernel,
        grid=(m // tm, d // tn),
        in_specs=[
            pl.BlockSpec((tm, tn), lambda i, j: (i, j)),
            pl.BlockSpec((tm, y.shape[1]), lambda i, j: (i, 0)),
            pl.BlockSpec((y.shape[1], tn), lambda i, j: (0, j)),
        ],
        out_specs=pl.BlockSpec((tm, tn), lambda i, j: (i, j)),
        out_shape=jax.ShapeDtypeStruct((m, d), F32),
        input_output_aliases={0: 0},
        compiler_params=_cparams("parallel", "parallel"),
        name="resid_proj",
    )(x2, y, w)


def _ffn_kernel(x_ref, g_ref, wg_ref, wu_ref, wd_ref, o_ref, xn_ref):
    @pl.when(pl.program_id(1) == 0)
    def _():
        x = x_ref[...]
        xn_ref[...] = _rms_rows(x, g_ref[...]).astype(BF16)
        o_ref[...] = x

    xn = xn_ref[...]
    a = _dot(xn, wg_ref[...])
    u = _dot(xn, wu_ref[...])
    hid = (a * _sigmoid(a) * u).astype(BF16)
    o_ref[...] += _dot(hid, wd_ref[...])


def _ffn(x2, g, wg, wu, wd):
    m, d = x2.shape
    f = wg.shape[1]
    tm = min(TM_FFN, m)
    tf = min(TF_FFN, f)
    return pl.pallas_call(
        _ffn_kernel,
        grid=(m // tm, f // tf),
        in_specs=[
            pl.BlockSpec((tm, d), lambda i, j: (i, 0)),
            pl.BlockSpec((1, d), lambda i, j: (0, 0)),
            pl.BlockSpec((d, tf), lambda i, j: (0, j)),
            pl.BlockSpec((d, tf), lambda i, j: (0, j)),
            pl.BlockSpec((tf, d), lambda i, j: (j, 0)),
        ],
        out_specs=pl.BlockSpec((tm, d), lambda i, j: (i, 0)),
        out_shape=jax.ShapeDtypeStruct((m, d), F32),
        scratch_shapes=[pltpu.VMEM((tm, d), BF16)],
        input_output_aliases={0: 0},
        compiler_params=_cparams("parallel", "arbitrary"),
        name="ffn",
    )(x2, g, wg, wu, wd)


def _row(v):
    return v.astype(F32).reshape(1, -1)


def kernel(x, norm_mix, w_in, b_forget, q_norm_diff, k_norm_diff, lambda_q1, lambda_k1, lambda_q2, lambda_k2, sub_norm_diff, q_norm_fox, k_norm_fox, w_branch_sb, w_branch_diff, w_branch_fox, w_gate, w_out, norm_ffn, w_ff_gate, w_ff_up, w_ff_down):
    b, s, d = x.shape
    depth = w_in.shape[0]
    d_main = HEAD_DIM * 3 * (H_SB + H_DIFF + H_FOX)
    slopes = 2.0 ** (-ALIBI_MAX * jnp.arange(1, H_DIFF + 1, dtype=F32) / H_DIFF)
    slopes = jnp.broadcast_to(slopes[:, None, None], (H_DIFF, 1, LANES))

    x2 = x.reshape(b * s, d)
    for l in range(depth):
        lambda_init = 0.8 - 0.6 * float(np.exp(-0.3 * l))
        g_mix = _row(norm_mix[l])
        w_main = w_in[l, :, :d_main].astype(BF16)
        w_f = jnp.pad(w_in[l, :, d_main:], ((0, 0), (0, LANES - H_FOX))).astype(BF16)
        b_pad = jnp.pad(_row(b_forget[l]), ((0, 0), (0, LANES - H_FOX)))

        h = _norm_proj(x2, g_mix, w_main, act=None, out_dtype=BF16, tn=TN_PROJ)
        hf = _norm_proj(x2, g_mix, w_f, act=None, out_dtype=F32, tn=LANES)
        gates = _norm_proj(x2, g_mix, w_gate[l].astype(BF16), act="sigmoid", out_dtype=BF16, tn=TN_PROJ)

        cum = _forget_cum(hf.reshape(b, s, LANES), b_pad)
        h3 = h.reshape(b, s, d_main)

        o_a = _stick_breaking(h3, b, s)
        lam_params = jnp.stack([lambda_q1[l], lambda_k1[l], lambda_q2[l], lambda_k2[l]]).astype(F32)
        o_b = _diff_attention(h3, b, s,
                              _row(jnp.tile(q_norm_diff[l], 2)), _row(jnp.tile(k_norm_diff[l], 2)),
                              _row(sub_norm_diff[l]), lam_params, slopes, lambda_init)
        o_c = _fox_attention(h3, b, s, _row(q_norm_fox[l]), _row(k_norm_fox[l]),
                             cum.reshape(b, SUBLANES, 1, s))

        y = _gated_merge(o_a.reshape(b * s, -1), o_b.reshape(b * s, -1), o_c.reshape(b * s, -1), gates,
                         w_branch_sb[l].astype(BF16), w_branch_diff[l].astype(BF16),
                         w_branch_fox[l].astype(BF16))
        x2 = _resid_proj(x2, y, w_out[l].astype(BF16))
        x2 = _ffn(x2, _row(norm_ffn[l]), w_ff_gate[l].astype(BF16), w_ff_up[l].astype(BF16),
                  w_ff_down[l].astype(BF16))
    return x2.reshape(b, s, d)
```

```python
import functools

import jax
import jax.numpy as jnp
import numpy as np
from jax import lax
from jax.experimental import pallas as pl
from jax.experimental.pallas import tpu as pltpu

F32 = jnp.float32
BF16 = jnp.bfloat16

HEAD_DIM = 128
DIFF_DIM = HEAD_DIM // 2
H_SB = 6
H_DIFF = 5
H_FOX = 5
CHUNK = 64
EPS = 1e-6
ALIBI_MAX = 8.0

LANES = 128
SUBLANES = 8
VMEM_LIMIT_BYTES = 56 * 1024 * 1024

TM_PROJ = 1024
TN_PROJ = 512
TM_FFN = 512
TF_FFN = 512
TQ = 256
TK = 256
NEG = -1e30
LOG2E = float(np.log2(np.e))


def _cparams(*sem):
    return pltpu.CompilerParams(dimension_semantics=sem, vmem_limit_bytes=VMEM_LIMIT_BYTES)


def _rms_rows(x, g):
    ms = jnp.mean(x * x, axis=-1, keepdims=True)
    return x * lax.rsqrt(ms + EPS) * g


def _softplus(z):
    return jnp.maximum(z, 0.0) + jnp.log(1.0 + jnp.exp(-jnp.abs(z)))


def _sigmoid(z):
    return 1.0 / (1.0 + jnp.exp(-z))


def _dot_nt(a, b):
    return lax.dot_general(a, b, (((1,), (1,)), ((), ())), preferred_element_type=F32)


def _dot(a, b):
    return jnp.dot(a, b, preferred_element_type=F32)


def _norm_proj_kernel(x_ref, g_ref, w_ref, o_ref, xn_ref, *, act):
    @pl.when(pl.program_id(1) == 0)
    def _():
        xn_ref[...] = _rms_rows(x_ref[...], g_ref[...]).astype(BF16)

    acc = _dot(xn_ref[...], w_ref[...])
    if act == "sigmoid":
        acc = _sigmoid(acc)
    o_ref[...] = acc.astype(o_ref.dtype)


def _norm_proj(x2, g, w, *, act, out_dtype, tn):
    m, d = x2.shape
    n = w.shape[1]
    tm = min(TM_PROJ, m)
    return pl.pallas_call(
        functools.partial(_norm_proj_kernel, act=act),
        grid=(m // tm, n // tn),
        in_specs=[
            pl.BlockSpec((tm, d), lambda i, j: (i, 0)),
            pl.BlockSpec((1, d), lambda i, j: (0, 0)),
            pl.BlockSpec((d, tn), lambda i, j: (0, j)),
        ],
        out_specs=pl.BlockSpec((tm, tn), lambda i, j: (i, j)),
        out_shape=jax.ShapeDtypeStruct((m, n), out_dtype),
        scratch_shapes=[pltpu.VMEM((tm, d), BF16)],
        compiler_params=_cparams("parallel", "arbitrary"),
        name="norm_proj_" + str(act),
    )(x2, g, w)


def _forget_cum_kernel(hf_ref, b_ref, o_ref, *, blk):
    s = hf_ref.shape[1]
    z = hf_ref[0] + b_ref[...]
    lf = jnp.minimum(z, 0.0) - jnp.log(1.0 + jnp.exp(-jnp.abs(z)))
    lft = lf.T[:SUBLANES]
    row = lax.broadcasted_iota(jnp.int32, (blk, blk), 0)
    col = lax.broadcasted_iota(jnp.int32, (blk, blk), 1)
    upper = (row <= col).astype(BF16)
    carry = jnp.zeros((SUBLANES, 1), F32)
    for i in range(s // blk):
        seg = lft[:, i * blk:(i + 1) * blk]
        p0 = seg.astype(BF16)
        r0 = seg - p0.astype(F32)
        p1 = r0.astype(BF16)
        p2 = (r0 - p1.astype(F32)).astype(BF16)
        cs = _dot(p0, upper) + _dot(p1, upper) + _dot(p2, upper) + carry
        o_ref[0, :, i * blk:(i + 1) * blk] = cs
        carry = cs[:, blk - 1:blk]


def _forget_cum(hf, b_pad):
    b, s, _ = hf.shape
    blk = min(256, s)
    return pl.pallas_call(
        functools.partial(_forget_cum_kernel, blk=blk),
        grid=(b,),
        in_specs=[
            pl.BlockSpec((1, s, LANES), lambda i: (i, 0, 0)),
            pl.BlockSpec((1, LANES), lambda i: (0, 0)),
        ],
        out_specs=pl.BlockSpec((1, SUBLANES, s), lambda i: (i, 0, 0)),
        out_shape=jax.ShapeDtypeStruct((b, SUBLANES, s), F32),
        compiler_params=_cparams("parallel"),
        name="forget_cum",
    )(hf, b_pad)


def _sb_kernel(q_ref, k_ref, v_ref, o_ref, *, tq):
    s = q_ref.shape[1]
    row = lax.broadcasted_iota(jnp.int32, (tq, tq), 0)
    col = lax.broadcasted_iota(jnp.int32, (tq, tq), 1)
    neg_tri = -((row >= col).astype(BF16))
    before = col < row

    for i in range(s // tq):
        q0 = i * tq
        nb = i + 1
        q = q_ref[0, q0:q0 + tq, :]
        z = _dot_nt(q, k_ref[0, 0:q0 + tq, :])
        zs = [z[:, j * tq:(j + 1) * tq] for j in range(nb)]
        sps = [jnp.maximum(zj, 0.0) + jnp.log2(1.0 + jnp.exp2(-jnp.abs(zj))) for zj in zs]
        sps[i] = jnp.where(before, sps[i], 0.0)
        his = [sp.astype(BF16) for sp in sps]
        los = [(sp - hi.astype(F32)).astype(BF16) for sp, hi in zip(sps, his)]
        suf = _dot(jnp.concatenate(his + los, axis=0), neg_tri)
        sufs = [suf[j * tq:(j + 1) * tq] + suf[(nb + j) * tq:(nb + j + 1) * tq] for j in range(nb)]
        carry = None
        ws = [None] * nb
        for j in reversed(range(nb)):
            arg = zs[j] + sufs[j]
            if carry is not None:
                arg = arg + carry
            if j == i:
                arg = jnp.where(before, arg, NEG)
            ws[j] = jnp.exp2(arg).astype(BF16)
            tot = sufs[j][:, 0:1]
            carry = tot if carry is None else carry + tot
        w = jnp.concatenate(ws, axis=1) if nb > 1 else ws[0]
        o_ref[0, q0:q0 + tq, :] = _dot(w, v_ref[0, 0:q0 + tq, :]).astype(o_ref.dtype)


def _stick_breaking(h3, b, s):
    tq = min(TQ, s)
    blk = lambda off: pl.BlockSpec((1, s, HEAD_DIM), lambda bi, hi: (bi, 0, off + hi))
    return pl.pallas_call(
        functools.partial(_sb_kernel, tq=tq),
        grid=(b, H_SB),
        in_specs=[blk(0), blk(H_SB), blk(2 * H_SB)],
        out_specs=pl.BlockSpec((1, s, HEAD_DIM), lambda bi, hi: (bi, 0, hi)),
        out_shape=jax.ShapeDtypeStruct((b, s, H_SB * HEAD_DIM), BF16),
        compiler_params=_cparams("parallel", "parallel"),
        name="stick_breaking",
    )(h3, h3, h3)


def _halves_rms(x, g2, lo_lane):
    x2 = x * x
    s_lo = jnp.sum(jnp.where(lo_lane, x2, 0.0), axis=-1, keepdims=True)
    s_hi = jnp.sum(jnp.where(lo_lane, 0.0, x2), axis=-1, keepdims=True)
    inv = jnp.where(lo_lane, lax.rsqrt(s_lo / DIFF_DIM + EPS), lax.rsqrt(s_hi / DIFF_DIM + EPS))
    return x * inv * g2


def _diff_kernel(q_ref, k_ref, v_ref, gq_ref, gk_ref, gs_ref, lam_ref, slope_ref, o_ref,
                 kn_ref, *, tq, lambda_init):
    s = q_ref.shape[1]
    lo_lane = lax.broadcasted_iota(jnp.int32, (1, HEAD_DIM), 1) < DIFF_DIM
    row = lax.broadcasted_iota(jnp.int32, (tq, tq), 0)
    col = lax.broadcasted_iota(jnp.int32, (tq, tq), 1)
    visible = (col // CHUNK) <= (row // CHUNK)
    slope = slope_ref[0][:, 0:1] * LOG2E
    diag_bias = (row - jnp.abs(row - col)).astype(F32) * slope
    key_pos = lax.broadcasted_iota(jnp.int32, (1, s), 1).astype(F32) * slope
    lp = lam_ref[...]
    lam = (jnp.exp(jnp.sum(lp[0:1] * lp[1:2], axis=-1, keepdims=True))
           - jnp.exp(jnp.sum(lp[2:3] * lp[3:4], axis=-1, keepdims=True)) + lambda_init)

    kn_ref[...] = _halves_rms(k_ref[0].astype(F32), gk_ref[...], lo_lane).astype(BF16)

    def rowmax(x):
        return jnp.max(x, axis=-1, keepdims=True)

    def rowsum(x):
        return jnp.sum(x, axis=-1, keepdims=True)

    for i in range(s // tq):
        q0 = i * tq
        qn = _halves_rms(q_ref[0, q0:q0 + tq, :].astype(F32), gq_ref[...], lo_lane) * (DIFF_DIM ** -0.5 * LOG2E)
        q12 = jnp.concatenate([jnp.where(lo_lane, qn, 0.0), jnp.where(lo_lane, 0.0, qn)], axis=0).astype(BF16)
        zd = _dot_nt(q12, kn_ref[q0:q0 + tq, :])
        s1d = jnp.where(visible, zd[:tq] + diag_bias, NEG)
        s2d = jnp.where(visible, zd[tq:] + diag_bias, NEG)
        m1, m2 = rowmax(s1d), rowmax(s2d)
        if i > 0:
            zo = _dot_nt(q12, kn_ref[0:q0, :])
            bo = key_pos[:, :q0] - slope * q0
            s1o = zo[:tq] + bo
            s2o = zo[tq:] + bo
            m1, m2 = jnp.maximum(m1, rowmax(s1o)), jnp.maximum(m2, rowmax(s2o))
        p1d, p2d = jnp.exp2(s1d - m1), jnp.exp2(s2d - m2)
        l1, l2 = rowsum(p1d), rowsum(p2d)
        if i > 0:
            p1o, p2o = jnp.exp2(s1o - m1), jnp.exp2(s2o - m2)
            l1, l2 = l1 + rowsum(p1o), l2 + rowsum(p2o)
        c1, c2 = 1.0 / l1, lam / l2
        acc = _dot((p1d * c1 - p2d * c2).astype(BF16), v_ref[0, q0:q0 + tq, :])
        if i > 0:
            acc = acc + _dot((p1o * c1 - p2o * c2).astype(BF16), v_ref[0, 0:q0, :])
        o = _rms_rows(acc, gs_ref[...]) * (1.0 - lambda_init)
        o_ref[0, q0:q0 + tq, :] = o.astype(o_ref.dtype)


def _diff_attention(h3, b, s, gq2, gk2, gs, lam_params, slopes, lambda_init):
    tq = min(TQ, s)
    off = 3 * H_SB
    blk = lambda o: pl.BlockSpec((1, s, HEAD_DIM), lambda bi, hi: (bi, 0, o + hi))
    vec = pl.BlockSpec((1, HEAD_DIM), lambda bi, hi: (0, 0))
    return pl.pallas_call(
        functools.partial(_diff_kernel, tq=tq, lambda_init=lambda_init),
        grid=(b, H_DIFF),
        in_specs=[blk(off), blk(off + H_DIFF), blk(off + 2 * H_DIFF), vec, vec, vec,
                  pl.BlockSpec((4, DIFF_DIM), lambda bi, hi: (0, 0)),
                  pl.BlockSpec((1, 1, LANES), lambda bi, hi: (hi, 0, 0))],
        out_specs=pl.BlockSpec((1, s, HEAD_DIM), lambda bi, hi: (bi, 0, hi)),
        out_shape=jax.ShapeDtypeStruct((b, s, H_DIFF * HEAD_DIM), BF16),
        scratch_shapes=[pltpu.VMEM((s, HEAD_DIM), BF16)],
        compiler_params=_cparams("parallel", "parallel"),
        name="diff_attention",
    )(h3, h3, h3, gq2, gk2, gs, lam_params, slopes)


def _fox_kernel(q_ref, k_ref, v_ref, gq_ref, gk_ref, cum_ref, o_ref, kn_ref, *, tq):
    s = q_ref.shape[1]
    scale = HEAD_DIM ** -0.5
    row = lax.broadcasted_iota(jnp.int32, (tq, tq), 0)
    col = lax.broadcasted_iota(jnp.int32, (tq, tq), 1)
    causal = col <= row

    kn_ref[...] = _rms_rows(k_ref[0].astype(F32), gk_ref[...]).astype(BF16)

    for i in range(s // tq):
        q0 = i * tq
        q = (_rms_rows(q_ref[0, q0:q0 + tq, :].astype(F32), gq_ref[...]) * (scale * LOG2E)).astype(BF16)
        cum = cum_ref[0, 0, :, 0:q0 + tq]
        bias = (cum[:, q0:q0 + 1] - cum) * LOG2E
        sd = jnp.where(causal, _dot_nt(q, kn_ref[q0:q0 + tq, :]) + bias[:, q0:], NEG)
        m = jnp.max(sd, axis=-1, keepdims=True)
        if i > 0:
            so = _dot_nt(q, kn_ref[0:q0, :]) + bias[:, :q0]
            m = jnp.maximum(m, jnp.max(so, axis=-1, keepdims=True))
        pd = jnp.exp2(sd - m)
        l = jnp.sum(pd, axis=-1, keepdims=True)
        acc = _dot(pd.astype(BF16), v_ref[0, q0:q0 + tq, :])
        if i > 0:
            po = jnp.exp2(so - m)
            l = l + jnp.sum(po, axis=-1, keepdims=True)
            acc = acc + _dot(po.astype(BF16), v_ref[0, 0:q0, :])
        o_ref[0, q0:q0 + tq, :] = (acc / l).astype(o_ref.dtype)


def _fox_attention(h3, b, s, gq, gk, cum4):
    tq = min(TQ, s)
    off = 3 * H_SB + 3 * H_DIFF
    blk = lambda o: pl.BlockSpec((1, s, HEAD_DIM), lambda bi, hi: (bi, 0, o + hi))
    vec = pl.BlockSpec((1, HEAD_DIM), lambda bi, hi: (0, 0))
    return pl.pallas_call(
        functools.partial(_fox_kernel, tq=tq),
        grid=(b, H_FOX),
        in_specs=[blk(off), blk(off + H_FOX), blk(off + 2 * H_FOX), vec, vec,
                  pl.BlockSpec((1, 1, 1, s), lambda bi, hi: (bi, hi, 0, 0))],
        out_specs=pl.BlockSpec((1, s, HEAD_DIM), lambda bi, hi: (bi, 0, hi)),
        out_shape=jax.ShapeDtypeStruct((b, s, H_FOX * HEAD_DIM), BF16),
        scratch_shapes=[pltpu.VMEM((s, HEAD_DIM), BF16)],
        compiler_params=_cparams("parallel", "parallel"),
        name="fox_attention",
    )(h3, h3, h3, gq, gk, cum4)


def _merge_kernel(oa_ref, ob_ref, oc_ref, ga_ref, gb_ref, gc_ref, wa_ref, wb_ref, wc_ref, y_ref):
    y = ga_ref[...].astype(F32) * _dot(oa_ref[...], wa_ref[...])
    y += gb_ref[...].astype(F32) * _dot(ob_ref[...], wb_ref[...])
    y += gc_ref[...].astype(F32) * _dot(oc_ref[...], wc_ref[...])
    y_ref[...] = y.astype(y_ref.dtype)


def _gated_merge(oa, ob, oc, gates, wa, wb, wc):
    m = oa.shape[0]
    d = wa.shape[1]
    tm = min(TM_PROJ, m)
    tn = min(TN_PROJ, d)
    nb = d // tn
    o_spec = lambda a: pl.BlockSpec((tm, a.shape[1]), lambda i, j: (i, 0))
    g_spec = lambda br: pl.BlockSpec((tm, tn), lambda i, j: (i, br * nb + j))
    w_spec = lambda w: pl.BlockSpec((w.shape[0], tn), lambda i, j: (0, j))
    return pl.pallas_call(
        _merge_kernel,
        grid=(m // tm, nb),
        in_specs=[o_spec(oa), o_spec(ob), o_spec(oc), g_spec(0), g_spec(1), g_spec(2),
                  w_spec(wa), w_spec(wb), w_spec(wc)],
        out_specs=pl.BlockSpec((tm, tn), lambda i, j: (i, j)),
        out_shape=jax.ShapeDtypeStruct((m, d), BF16),
        compiler_params=_cparams("parallel", "parallel"),
        name="gated_merge",
    )(oa, ob, oc, gates, gates, gates, wa, wb, wc)


def _resid_proj_kernel(x_ref, y_ref, w_ref, o_ref):
    o_ref[...] = x_ref[...] + _dot(y_ref[...], w_ref[...])


def _resid_proj(x2, y, w):
    m, d = x2.shape
    tm = min(TM_PROJ, m)
    tn = min(TN_PROJ, d)
    return pl.pallas_call(
        _resid_proj_kernel,
        grid=(m // tm, d // tn),
        in_specs=[
            pl.BlockSpec((tm, tn), lambda i, j: (i, j)),
            pl.BlockSpec((tm, y.shape[1]), lambda i, j: (i, 0)),
            pl.BlockSpec((y.shape[1], tn), lambda i, j: (0, j)),
        ],
        out_specs=pl.BlockSpec((tm, tn), lambda i, j: (i, j)),
        out_shape=jax.ShapeDtypeStruct((m, d), F32),
        input_output_aliases={0: 0},
        compiler_params=_cparams("parallel", "parallel"),
        name="resid_proj",
    )(x2, y, w)


def _ffn_kernel(x_ref, g_ref, wg_ref, wu_ref, wd_ref, o_ref, xn_ref):
    @pl.when(pl.program_id(1) == 0)
    def _():
        x = x_ref[...]
        xn_ref[...] = _rms_rows(x, g_ref[...]).astype(BF16)
        o_ref[...] = x

    xn = xn_ref[...]
    a = _dot(xn, wg_ref[...])
    u = _dot(xn, wu_ref[...])
    hid = (a * _sigmoid(a) * u).astype(BF16)
    o_ref[...] += _dot(hid, wd_ref[...])


def _ffn(x2, g, wg, wu, wd):
    m, d = x2.shape
    f = wg.shape[1]
    tm = min(TM_FFN, m)
    tf = min(TF_FFN, f)
    return pl.pallas_call(
        _ffn_kernel,
        grid=(m // tm, f // tf),
        in_specs=[
            pl.BlockSpec((tm, d), lambda i, j: (i, 0)),
            pl.BlockSpec((1, d), lambda i, j: (0, 0)),
            pl.BlockSpec((d, tf), lambda i, j: (0, j)),
            pl.BlockSpec((d, tf), lambda i, j: (0, j)),
            pl.BlockSpec((tf, d), lambda i, j: (j, 0)),
        ],
        out_specs=pl.BlockSpec((tm, d), lambda i, j: (i, 0)),
        out_shape=jax.ShapeDtypeStruct((m, d), F32),
        scratch_shapes=[pltpu.VMEM((tm, d), BF16)],
        input_output_aliases={0: 0},
        compiler_params=_cparams("parallel", "arbitrary"),
        name="ffn",
    )(x2, g, wg, wu, wd)


def _row(v):
    return v.astype(F32).reshape(1, -1)


def kernel(x, norm_mix, w_in, b_forget, q_norm_diff, k_norm_diff, lambda_q1, lambda_k1, lambda_q2, lambda_k2, sub_norm_diff, q_norm_fox, k_norm_fox, w_branch_sb, w_branch_diff, w_branch_fox, w_gate, w_out, norm_ffn, w_ff_gate, w_ff_up, w_ff_down):
    b, s, d = x.shape
    depth = w_in.shape[0]
    d_main = HEAD_DIM * 3 * (H_SB + H_DIFF + H_FOX)
    slopes = 2.0 ** (-ALIBI_MAX * jnp.arange(1, H_DIFF + 1, dtype=F32) / H_DIFF)
    slopes = jnp.broadcast_to(slopes[:, None, None], (H_DIFF, 1, LANES))

    x2 = x.reshape(b * s, d)
    for l in range(depth):
        lambda_init = 0.8 - 0.6 * float(np.exp(-0.3 * l))
        g_mix = _row(norm_mix[l])
        d_sb = H_SB * HEAD_DIM
        w_main = jnp.concatenate([w_in[l, :, :d_sb] * (HEAD_DIM ** -0.5 * LOG2E), w_in[l, :, d_sb:d_main]],
                                 axis=1).astype(BF16)
        w_f = jnp.pad(w_in[l, :, d_main:], ((0, 0), (0, LANES - H_FOX))).astype(BF16)
        b_pad = jnp.pad(_row(b_forget[l]), ((0, 0), (0, LANES - H_FOX)))

        h = _norm_proj(x2, g_mix, w_main, act=None, out_dtype=BF16, tn=TN_PROJ)
        hf = _norm_proj(x2, g_mix, w_f, act=None, out_dtype=F32, tn=LANES)
        gates = _norm_proj(x2, g_mix, w_gate[l].astype(BF16), act="sigmoid", out_dtype=BF16, tn=TN_PROJ)

        cum = _forget_cum(hf.reshape(b, s, LANES), b_pad)
        h3 = h.reshape(b, s, d_main)

        o_a = _stick_breaking(h3, b, s)
        lam_params = jnp.stack([lambda_q1[l], lambda_k1[l], lambda_q2[l], lambda_k2[l]]).astype(F32)
        o_b = _diff_attention(h3, b, s,
                              _row(jnp.tile(q_norm_diff[l], 2)), _row(jnp.tile(k_norm_diff[l], 2)),
                              _row(sub_norm_diff[l]), lam_params, slopes, lambda_init)
        o_c = _fox_attention(h3, b, s, _row(q_norm_fox[l]), _row(k_norm_fox[l]),
                             cum.reshape(b, SUBLANES, 1, s))

        y = _gated_merge(o_a.reshape(b * s, -1), o_b.reshape(b * s, -1), o_c.reshape(b * s, -1), gates,
                         w_branch_sb[l].astype(BF16), w_branch_diff[l].astype(BF16),
                         w_branch_fox[l].astype(BF16))
        x2 = _resid_proj(x2, y, w_out[l].astype(BF16))
        x2 = _ffn(x2, _row(norm_ffn[l]), w_ff_gate[l].astype(BF16), w_ff_up[l].astype(BF16),
                  w_ff_down[l].astype(BF16))
    return x2.reshape(b, s, d)
```

```python
import functools

import jax
import jax.numpy as jnp
import numpy as np
from jax import lax
from jax.experimental import pallas as pl
from jax.experimental.pallas import tpu as pltpu

F32 = jnp.float32
BF16 = jnp.bfloat16

HEAD_DIM = 128
DIFF_DIM = HEAD_DIM // 2
H_SB = 6
H_DIFF = 5
H_FOX = 5
CHUNK = 64
EPS = 1e-6
ALIBI_MAX = 8.0

LANES = 128
SUBLANES = 8
VMEM_LIMIT_BYTES = 56 * 1024 * 1024

TM_PROJ = 1024
TN_PROJ = 512
TN_IN_PROJ = 1024
PROJ_CHUNK = 256
TM_FFN = 512
TF_FFN = 512
TQ = 256
TK = 256
INTERLEAVE = 3
NEG = -1e30
LOG2E = float(np.log2(np.e))


def _cparams(*sem):
    return pltpu.CompilerParams(dimension_semantics=sem, vmem_limit_bytes=VMEM_LIMIT_BYTES)


def _rms_rows(x, g):
    ms = jnp.mean(x * x, axis=-1, keepdims=True)
    return x * lax.rsqrt(ms + EPS) * g


def _softplus(z):
    return jnp.maximum(z, 0.0) + jnp.log(1.0 + jnp.exp(-jnp.abs(z)))


def _sigmoid(z):
    return 0.5 * jnp.tanh(0.5 * z) + 0.5


def _silu(z):
    h = 0.5 * z
    return h * jnp.tanh(h) + h


def _interleave(tasks, width):
    tasks = iter(tasks)
    active = [t for _, t in zip(range(width), tasks)]
    while active:
        for t in list(active):
            try:
                next(t)
            except StopIteration:
                active.remove(t)
                nxt = next(tasks, None)
                if nxt is not None:
                    active.append(nxt)


def _dot_nt(a, b):
    return lax.dot_general(a, b, (((1,), (1,)), ((), ())), preferred_element_type=F32)


def _dot(a, b):
    return jnp.dot(a, b, preferred_element_type=F32)


def _in_proj_kernel(x_ref, g_ref, w_ref, wf_ref, o_ref, hf_ref, xn_ref, *, n_plain):
    j = pl.program_id(1)

    @pl.when(j == 0)
    def _():
        xn_ref[...] = _rms_rows(x_ref[...], g_ref[...]).astype(BF16)
        hf_ref[...] = _dot(xn_ref[...], wf_ref[...])

    def project(act):
        for c in range(0, o_ref.shape[1], PROJ_CHUNK):
            acc = _dot(xn_ref[...], w_ref[:, c:c + PROJ_CHUNK])
            o_ref[:, c:c + PROJ_CHUNK] = act(acc).astype(o_ref.dtype)

    @pl.when(j < n_plain)
    def _():
        project(lambda a: a)

    @pl.when(j >= n_plain)
    def _():
        project(_sigmoid)


def _in_proj(x2, g, w, w_f, *, d_plain):
    m, d = x2.shape
    n = w.shape[1]
    tm = min(TM_PROJ, m)
    tn = TN_IN_PROJ if (n % TN_IN_PROJ == 0 and d_plain % TN_IN_PROJ == 0) else TN_PROJ
    return pl.pallas_call(
        functools.partial(_in_proj_kernel, n_plain=d_plain // tn),
        grid=(m // tm, n // tn),
        in_specs=[
            pl.BlockSpec((tm, d), lambda i, j: (i, 0)),
            pl.BlockSpec((1, d), lambda i, j: (0, 0)),
            pl.BlockSpec((d, tn), lambda i, j: (0, j)),
            pl.BlockSpec((d, LANES), lambda i, j: (0, 0)),
        ],
        out_specs=[pl.BlockSpec((tm, tn), lambda i, j: (i, j)),
                   pl.BlockSpec((tm, LANES), lambda i, j: (i, 0))],
        out_shape=[jax.ShapeDtypeStruct((m, n), BF16), jax.ShapeDtypeStruct((m, LANES), F32)],
        scratch_shapes=[pltpu.VMEM((tm, d), BF16)],
        compiler_params=_cparams("parallel", "arbitrary"),
        name="in_proj",
    )(x2, g, w, w_f)


def _forget_cum_kernel(hf_ref, b_ref, o_ref, *, blk):
    s = hf_ref.shape[1]
    z = hf_ref[0] + b_ref[...]
    lf = jnp.minimum(z, 0.0) - jnp.log(1.0 + jnp.exp(-jnp.abs(z)))
    lft = lf.T[:SUBLANES]
    row = lax.broadcasted_iota(jnp.int32, (blk, blk), 0)
    col = lax.broadcasted_iota(jnp.int32, (blk, blk), 1)
    upper = (row <= col).astype(BF16)
    carry = jnp.zeros((SUBLANES, 1), F32)
    for i in range(s // blk):
        seg = lft[:, i * blk:(i + 1) * blk]
        p0 = seg.astype(BF16)
        r0 = seg - p0.astype(F32)
        p1 = r0.astype(BF16)
        p2 = (r0 - p1.astype(F32)).astype(BF16)
        cs = _dot(p0, upper) + _dot(p1, upper) + _dot(p2, upper) + carry
        o_ref[0, :, i * blk:(i + 1) * blk] = cs
        carry = cs[:, blk - 1:blk]


def _forget_cum(hf, b_pad):
    b, s, _ = hf.shape
    blk = min(256, s)
    return pl.pallas_call(
        functools.partial(_forget_cum_kernel, blk=blk),
        grid=(b,),
        in_specs=[
            pl.BlockSpec((1, s, LANES), lambda i: (i, 0, 0)),
            pl.BlockSpec((1, LANES), lambda i: (0, 0)),
        ],
        out_specs=pl.BlockSpec((1, SUBLANES, s), lambda i: (i, 0, 0)),
        out_shape=jax.ShapeDtypeStruct((b, SUBLANES, s), F32),
        compiler_params=_cparams("parallel"),
        name="forget_cum",
    )(hf, b_pad)


def _sb_kernel(q_ref, k_ref, v_ref, o_ref, *, tq):
    s = q_ref.shape[1]
    row = lax.broadcasted_iota(jnp.int32, (tq, tq), 0)
    col = lax.broadcasted_iota(jnp.int32, (tq, tq), 1)
    neg_tri = -((row >= col).astype(BF16))
    before = col < row

    def q_block(i):
        q0 = i * tq
        q = q_ref[0, q0:q0 + tq, :]
        zs, sufs = [], []
        for j in range(i + 1):
            z = _dot_nt(q, k_ref[0, j * tq:(j + 1) * tq, :])
            sp = jnp.maximum(z, 0.0) + jnp.log2(1.0 + jnp.exp2(-jnp.abs(z)))
            if j == i:
                sp = jnp.where(before, sp, 0.0)
            zs.append(z)
            yield
            hi = sp.astype(BF16)
            lo = (sp - hi.astype(F32)).astype(BF16)
            sg = _dot(jnp.concatenate([hi, lo], axis=0), neg_tri)
            sufs.append(sg[:tq] + sg[tq:])
            yield
        carry = None
        acc = None
        for j in reversed(range(i + 1)):
            arg = zs[j] + sufs[j]
            if carry is not None:
                arg = arg + carry
            if j == i:
                arg = jnp.where(before, arg, NEG)
            pv = _dot(jnp.exp2(arg).astype(BF16), v_ref[0, j * tq:(j + 1) * tq, :])
            acc = pv if acc is None else acc + pv
            tot = sufs[j][:, 0:1]
            carry = tot if carry is None else carry + tot
            yield
        o_ref[0, q0:q0 + tq, :] = acc.astype(o_ref.dtype)

    _interleave([q_block(i) for i in range(s // tq)], INTERLEAVE)


def _stick_breaking(h3, b, s):
    tq = min(TQ, s)
    blk = lambda off: pl.BlockSpec((1, s, HEAD_DIM), lambda bi, hi: (bi, 0, off + hi))
    return pl.pallas_call(
        functools.partial(_sb_kernel, tq=tq),
        grid=(b, H_SB),
        in_specs=[blk(0), blk(H_SB), blk(2 * H_SB)],
        out_specs=pl.BlockSpec((1, s, HEAD_DIM), lambda bi, hi: (bi, 0, hi)),
        out_shape=jax.ShapeDtypeStruct((b, s, H_SB * HEAD_DIM), BF16),
        compiler_params=_cparams("parallel", "parallel"),
        name="stick_breaking",
    )(h3, h3, h3)


def _halves_rms(x, g2, lo_lane):
    x2 = x * x
    s_lo = jnp.sum(jnp.where(lo_lane, x2, 0.0), axis=-1, keepdims=True)
    s_hi = jnp.sum(jnp.where(lo_lane, 0.0, x2), axis=-1, keepdims=True)
    inv = jnp.where(lo_lane, lax.rsqrt(s_lo / DIFF_DIM + EPS), lax.rsqrt(s_hi / DIFF_DIM + EPS))
    return x * inv * g2


def _diff_kernel(q_ref, k_ref, v_ref, gq_ref, gk_ref, gs_ref, lam_ref, slope_ref, o_ref,
                 kn_ref, *, tq, lambda_init):
    s = q_ref.shape[1]
    lo_lane = lax.broadcasted_iota(jnp.int32, (1, HEAD_DIM), 1) < DIFF_DIM
    row = lax.broadcasted_iota(jnp.int32, (tq, tq), 0)
    col = lax.broadcasted_iota(jnp.int32, (tq, tq), 1)
    visible = (col // CHUNK) <= (row // CHUNK)
    slope = slope_ref[0][:, 0:1] * LOG2E
    diag_bias = (row - jnp.abs(row - col)).astype(F32) * slope
    key_pos = lax.broadcasted_iota(jnp.int32, (1, s), 1).astype(F32) * slope
    lp = lam_ref[...]
    lam = (jnp.exp(jnp.sum(lp[0:1] * lp[1:2], axis=-1, keepdims=True))
           - jnp.exp(jnp.sum(lp[2:3] * lp[3:4], axis=-1, keepdims=True)) + lambda_init)

    kn_ref[...] = _halves_rms(k_ref[0].astype(F32), gk_ref[...], lo_lane).astype(BF16)

    def rowmax(x):
        return jnp.max(x, axis=-1, keepdims=True)

    def rowsum(x):
        return jnp.sum(x, axis=-1, keepdims=True)

    def q_block(i):
        q0 = i * tq
        qn = _halves_rms(q_ref[0, q0:q0 + tq, :].astype(F32), gq_ref[...], lo_lane) * (DIFF_DIM ** -0.5 * LOG2E)
        q12 = jnp.concatenate([jnp.where(lo_lane, qn, 0.0), jnp.where(lo_lane, 0.0, qn)], axis=0).astype(BF16)
        bo = key_pos[:, :q0 + tq] - slope * q0
        s1, s2 = [], []
        for j in range(i + 1):
            z = _dot_nt(q12, kn_ref[j * tq:(j + 1) * tq, :])
            if j == i:
                s1.append(jnp.where(visible, z[:tq] + diag_bias, NEG))
                s2.append(jnp.where(visible, z[tq:] + diag_bias, NEG))
            else:
                s1.append(z[:tq] + bo[:, j * tq:(j + 1) * tq])
                s2.append(z[tq:] + bo[:, j * tq:(j + 1) * tq])
            yield
        m1 = rowmax(functools.reduce(jnp.maximum, s1))
        m2 = rowmax(functools.reduce(jnp.maximum, s2))
        p1, p2 = [], []
        for a, b2 in zip(s1, s2):
            p1.append(jnp.exp2(a - m1))
            p2.append(jnp.exp2(b2 - m2))
            yield
        c1 = 1.0 / rowsum(functools.reduce(jnp.add, p1))
        c2 = lam / rowsum(functools.reduce(jnp.add, p2))
        acc = None
        for j in range(i + 1):
            pv = _dot((p1[j] * c1 - p2[j] * c2).astype(BF16), v_ref[0, j * tq:(j + 1) * tq, :])
            acc = pv if acc is None else acc + pv
            yield
        o = _rms_rows(acc, gs_ref[...]) * (1.0 - lambda_init)
        o_ref[0, q0:q0 + tq, :] = o.astype(o_ref.dtype)

    _interleave([q_block(i) for i in range(s // tq)], INTERLEAVE)


def _diff_attention(h3, b, s, gq2, gk2, gs, lam_params, slopes, lambda_init):
    tq = min(TQ, s)
    off = 3 * H_SB
    blk = lambda o: pl.BlockSpec((1, s, HEAD_DIM), lambda bi, hi: (bi, 0, o + hi))
    vec = pl.BlockSpec((1, HEAD_DIM), lambda bi, hi: (0, 0))
    return pl.pallas_call(
        functools.partial(_diff_kernel, tq=tq, lambda_init=lambda_init),
        grid=(b, H_DIFF),
        in_specs=[blk(off), blk(off + H_DIFF), blk(off + 2 * H_DIFF), vec, vec, vec,
                  pl.BlockSpec((4, DIFF_DIM), lambda bi, hi: (0, 0)),
                  pl.BlockSpec((1, 1, LANES), lambda bi, hi: (hi, 0, 0))],
        out_specs=pl.BlockSpec((1, s, HEAD_DIM), lambda bi, hi: (bi, 0, hi)),
        out_shape=jax.ShapeDtypeStruct((b, s, H_DIFF * HEAD_DIM), BF16),
        scratch_shapes=[pltpu.VMEM((s, HEAD_DIM), BF16)],
        compiler_params=_cparams("parallel", "parallel"),
        name="diff_attention",
    )(h3, h3, h3, gq2, gk2, gs, lam_params, slopes)


def _fox_kernel(q_ref, k_ref, v_ref, gq_ref, gk_ref, cum_ref, o_ref, kn_ref, *, tq):
    s = q_ref.shape[1]
    scale = HEAD_DIM ** -0.5
    row = lax.broadcasted_iota(jnp.int32, (tq, tq), 0)
    col = lax.broadcasted_iota(jnp.int32, (tq, tq), 1)
    causal = col <= row

    kn_ref[...] = _rms_rows(k_ref[0].astype(F32), gk_ref[...]).astype(BF16)

    def q_block(i):
        q0 = i * tq
        q = (_rms_rows(q_ref[0, q0:q0 + tq, :].astype(F32), gq_ref[...]) * (scale * LOG2E)).astype(BF16)
        cum = cum_ref[0, 0, :, 0:q0 + tq]
        bias = (cum[:, q0:q0 + 1] - cum) * LOG2E
        ss = []
        for j in range(i + 1):
            sj = _dot_nt(q, kn_ref[j * tq:(j + 1) * tq, :]) + bias[:, j * tq:(j + 1) * tq]
            ss.append(jnp.where(causal, sj, NEG) if j == i else sj)
            yield
        m = jnp.max(functools.reduce(jnp.maximum, ss), axis=-1, keepdims=True)
        ps = []
        for sj in ss:
            ps.append(jnp.exp2(sj - m))
            yield
        l = jnp.sum(functools.reduce(jnp.add, ps), axis=-1, keepdims=True)
        acc = None
        for j, pj in enumerate(ps):
            pv = _dot(pj.astype(BF16), v_ref[0, j * tq:(j + 1) * tq, :])
            acc = pv if acc is None else acc + pv
            yield
        o_ref[0, q0:q0 + tq, :] = (acc / l).astype(o_ref.dtype)

    _interleave([q_block(i) for i in range(s // tq)], INTERLEAVE)


def _fox_attention(h3, b, s, gq, gk, cum4):
    tq = min(TQ, s)
    off = 3 * H_SB + 3 * H_DIFF
    blk = lambda o: pl.BlockSpec((1, s, HEAD_DIM), lambda bi, hi: (bi, 0, o + hi))
    vec = pl.BlockSpec((1, HEAD_DIM), lambda bi, hi: (0, 0))
    return pl.pallas_call(
        functools.partial(_fox_kernel, tq=tq),
        grid=(b, H_FOX),
        in_specs=[blk(off), blk(off + H_FOX), blk(off + 2 * H_FOX), vec, vec,
                  pl.BlockSpec((1, 1, 1, s), lambda bi, hi: (bi, hi, 0, 0))],
        out_specs=pl.BlockSpec((1, s, HEAD_DIM), lambda bi, hi: (bi, 0, hi)),
        out_shape=jax.ShapeDtypeStruct((b, s, H_FOX * HEAD_DIM), BF16),
        scratch_shapes=[pltpu.VMEM((s, HEAD_DIM), BF16)],
        compiler_params=_cparams("parallel", "parallel"),
        name="fox_attention",
    )(h3, h3, h3, gq, gk, cum4)


def _merge_kernel(oa_ref, ob_ref, oc_ref, ga_ref, gb_ref, gc_ref, wa_ref, wb_ref, wc_ref, y_ref):
    for c in range(0, y_ref.shape[1], PROJ_CHUNK):
        cs = slice(c, c + PROJ_CHUNK)
        y = ga_ref[:, cs].astype(F32) * _dot(oa_ref[...], wa_ref[:, cs])
        y += gb_ref[:, cs].astype(F32) * _dot(ob_ref[...], wb_ref[:, cs])
        y += gc_ref[:, cs].astype(F32) * _dot(oc_ref[...], wc_ref[:, cs])
        y_ref[:, cs] = y.astype(y_ref.dtype)


def _gated_merge(oa, ob, oc, hg, gate_col, wa, wb, wc):
    m = oa.shape[0]
    d = wa.shape[1]
    tm = min(TM_PROJ, m)
    tn = TN_IN_PROJ if d % TN_IN_PROJ == 0 else TN_PROJ
    nb = d // tn
    g0 = gate_col // tn
    o_spec = lambda a: pl.BlockSpec((tm, a.shape[1]), lambda i, j: (i, 0))
    g_spec = lambda br: pl.BlockSpec((tm, tn), lambda i, j: (i, g0 + br * nb + j))
    w_spec = lambda w: pl.BlockSpec((w.shape[0], tn), lambda i, j: (0, j))
    return pl.pallas_call(
        _merge_kernel,
        grid=(m // tm, nb),
        in_specs=[o_spec(oa), o_spec(ob), o_spec(oc), g_spec(0), g_spec(1), g_spec(2),
                  w_spec(wa), w_spec(wb), w_spec(wc)],
        out_specs=pl.BlockSpec((tm, tn), lambda i, j: (i, j)),
        out_shape=jax.ShapeDtypeStruct((m, d), BF16),
        compiler_params=_cparams("parallel", "parallel"),
        name="gated_merge",
    )(oa, ob, oc, hg, hg, hg, wa, wb, wc)


def _resid_proj_kernel(x_ref, y_ref, w_ref, o_ref):
    for c in range(0, o_ref.shape[1], PROJ_CHUNK):
        cs = slice(c, c + PROJ_CHUNK)
        o_ref[:, cs] = x_ref[:, cs] + _dot(y_ref[...], w_ref[:, cs])


def _resid_proj(x2, y, w):
    m, d = x2.shape
    tm = min(TM_PROJ, m)
    tn = TN_IN_PROJ if d % TN_IN_PROJ == 0 else TN_PROJ
    return pl.pallas_call(
        _resid_proj_kernel,
        grid=(m // tm, d // tn),
        in_specs=[
            pl.BlockSpec((tm, tn), lambda i, j: (i, j)),
            pl.BlockSpec((tm, y.shape[1]), lambda i, j: (i, 0)),
            pl.BlockSpec((y.shape[1], tn), lambda i, j: (0, j)),
        ],
        out_specs=pl.BlockSpec((tm, tn), lambda i, j: (i, j)),
        out_shape=jax.ShapeDtypeStruct((m, d), F32),
        input_output_aliases={0: 0},
        compiler_params=_cparams("parallel", "parallel"),
        name="resid_proj",
    )(x2, y, w)


def _ffn_kernel(x_ref, g_ref, wg_ref, wu_ref, wd_ref, o_ref, xn_ref):
    @pl.when(pl.program_id(1) == 0)
    def _():
        x = x_ref[...]
        xn_ref[...] = _rms_rows(x, g_ref[...]).astype(BF16)
        o_ref[...] = x

    xn = xn_ref[...]
    acc = None
    for c in range(0, wg_ref.shape[1], PROJ_CHUNK):
        a = _dot(xn, wg_ref[:, c:c + PROJ_CHUNK])
        u = _dot(xn, wu_ref[:, c:c + PROJ_CHUNK])
        part = _dot((_silu(a) * u).astype(BF16), wd_ref[c:c + PROJ_CHUNK, :])
        acc = part if acc is None else acc + part
    o_ref[...] += acc


def _ffn(x2, g, wg, wu, wd):
    m, d = x2.shape
    f = wg.shape[1]
    tm = min(TM_FFN, m)
    tf = min(TF_FFN, f)
    return pl.pallas_call(
        _ffn_kernel,
        grid=(m // tm, f // tf),
        in_specs=[
            pl.BlockSpec((tm, d), lambda i, j: (i, 0)),
            pl.BlockSpec((1, d), lambda i, j: (0, 0)),
            pl.BlockSpec((d, tf), lambda i, j: (0, j)),
            pl.BlockSpec((d, tf), lambda i, j: (0, j)),
            pl.BlockSpec((tf, d), lambda i, j: (j, 0)),
        ],
        out_specs=pl.BlockSpec((tm, d), lambda i, j: (i, 0)),
        out_shape=jax.ShapeDtypeStruct((m, d), F32),
        scratch_shapes=[pltpu.VMEM((tm, d), BF16)],
        input_output_aliases={0: 0},
        compiler_params=_cparams("parallel", "arbitrary"),
        name="ffn",
    )(x2, g, wg, wu, wd)


def _row(v):
    return v.astype(F32).reshape(1, -1)


def kernel(x, norm_mix, w_in, b_forget, q_norm_diff, k_norm_diff, lambda_q1, lambda_k1, lambda_q2, lambda_k2, sub_norm_diff, q_norm_fox, k_norm_fox, w_branch_sb, w_branch_diff, w_branch_fox, w_gate, w_out, norm_ffn, w_ff_gate, w_ff_up, w_ff_down):
    b, s, d = x.shape
    depth = w_in.shape[0]
    d_main = HEAD_DIM * 3 * (H_SB + H_DIFF + H_FOX)
    slopes = 2.0 ** (-ALIBI_MAX * jnp.arange(1, H_DIFF + 1, dtype=F32) / H_DIFF)
    slopes = jnp.broadcast_to(slopes[:, None, None], (H_DIFF, 1, LANES))

    x2 = x.reshape(b * s, d)
    for l in range(depth):
        lambda_init = 0.8 - 0.6 * float(np.exp(-0.3 * l))
        g_mix = _row(norm_mix[l])
        d_sb = H_SB * HEAD_DIM
        w_cat = jnp.concatenate([w_in[l, :, :d_sb] * (HEAD_DIM ** -0.5 * LOG2E), w_in[l, :, d_sb:d_main],
                                 w_gate[l]], axis=1).astype(BF16)
        w_f = jnp.pad(w_in[l, :, d_main:], ((0, 0), (0, LANES - H_FOX))).astype(BF16)
        b_pad = jnp.pad(_row(b_forget[l]), ((0, 0), (0, LANES - H_FOX)))

        hg, hf = _in_proj(x2, g_mix, w_cat, w_f, d_plain=d_main)

        cum = _forget_cum(hf.reshape(b, s, LANES), b_pad)
        h3 = hg.reshape(b, s, hg.shape[1])

        o_a = _stick_breaking(h3, b, s)
        lam_params = jnp.stack([lambda_q1[l], lambda_k1[l], lambda_q2[l], lambda_k2[l]]).astype(F32)
        o_b = _diff_attention(h3, b, s,
                              _row(jnp.tile(q_norm_diff[l], 2)), _row(jnp.tile(k_norm_diff[l], 2)),
                              _row(sub_norm_diff[l]), lam_params, slopes, lambda_init)
        o_c = _fox_attention(h3, b, s, _row(q_norm_fox[l]), _row(k_norm_fox[l]),
                             cum.reshape(b, SUBLANES, 1, s))

        y = _gated_merge(o_a.reshape(b * s, -1), o_b.reshape(b * s, -1), o_c.reshape(b * s, -1), hg, d_main,
                         w_branch_sb[l].astype(BF16), w_branch_diff[l].astype(BF16),
                         w_branch_fox[l].astype(BF16))
        x2 = _resid_proj(x2, y, w_out[l].astype(BF16))
        x2 = _ffn(x2, _row(norm_ffn[l]), w_ff_gate[l].astype(BF16), w_ff_up[l].astype(BF16),
                  w_ff_down[l].astype(BF16))
    return x2.reshape(b, s, d)
```

```python
import functools

import jax
import jax.numpy as jnp
import numpy as np
from jax import lax
from jax.experimental import pallas as pl
from jax.experimental.pallas import tpu as pltpu

F32 = jnp.float32
BF16 = jnp.bfloat16

HEAD_DIM = 128
DIFF_DIM = HEAD_DIM // 2
H_SB = 6
H_DIFF = 5
H_FOX = 5
CHUNK = 64
EPS = 1e-6
ALIBI_MAX = 8.0

LANES = 128
SUBLANES = 8
VMEM_LIMIT_BYTES = 56 * 1024 * 1024

TM_PROJ = 1024
TN_PROJ = 512
TN_IN_PROJ = 1024
PROJ_CHUNK = 256
TM_FFN = 1024
TF_FFN = 512
TQ = 256
TK = 256
INTERLEAVE_SB = 4
INTERLEAVE_DIFF = 3
INTERLEAVE_FOX = 3
NEG = -1e30
LOG2E = float(np.log2(np.e))


def _cparams(*sem):
    return pltpu.CompilerParams(dimension_semantics=sem, vmem_limit_bytes=VMEM_LIMIT_BYTES)


def _rms_rows(x, g):
    ms = jnp.mean(x * x, axis=-1, keepdims=True)
    return x * lax.rsqrt(ms + EPS) * g


def _softplus(z):
    return jnp.maximum(z, 0.0) + jnp.log(1.0 + jnp.exp(-jnp.abs(z)))


def _sigmoid(z):
    return 0.5 * jnp.tanh(0.5 * z) + 0.5


def _silu(z):
    h = 0.5 * z
    return h * jnp.tanh(h) + h


def _interleave(tasks, width):
    tasks = iter(tasks)
    active = [t for _, t in zip(range(width), tasks)]
    while active:
        for t in list(active):
            try:
                next(t)
            except StopIteration:
                active.remove(t)
                nxt = next(tasks, None)
                if nxt is not None:
                    active.append(nxt)


def _dot_nt(a, b):
    return lax.dot_general(a, b, (((1,), (1,)), ((), ())), preferred_element_type=F32)


def _dot(a, b):
    return jnp.dot(a, b, preferred_element_type=F32)


def _in_proj_kernel(x_ref, g_ref, cs_ref, w_ref, wg_ref, wf_ref, o_ref, hf_ref, xn_ref, *, n_plain):
    j = pl.program_id(1)

    @pl.when(j == 0)
    def _():
        xn_ref[...] = _rms_rows(x_ref[...], g_ref[...]).astype(BF16)
        hf_ref[...] = _dot(xn_ref[...], wf_ref[...])

    def project(wt_ref, act):
        for c in range(0, o_ref.shape[1], PROJ_CHUNK):
            cs = slice(c, c + PROJ_CHUNK)
            o_ref[:, cs] = act(_dot(xn_ref[...], wt_ref[:, cs]), cs).astype(o_ref.dtype)

    @pl.when(j < n_plain)
    def _():
        project(w_ref, lambda a, cs: a * cs_ref[:, cs])

    @pl.when(j >= n_plain)
    def _():
        project(wg_ref, lambda a, cs: _sigmoid(a))


def _in_proj(x2, g, col_scale, w_in, w_gate, w_f, layer):
    m, d = x2.shape
    d_plain, d_gate = col_scale.shape[1], w_gate.shape[2]
    tm = min(TM_PROJ, m)
    tn = TN_IN_PROJ if (d_gate % TN_IN_PROJ == 0 and d_plain % TN_IN_PROJ == 0) else TN_PROJ
    n_plain = d_plain // tn
    return pl.pallas_call(
        functools.partial(_in_proj_kernel, n_plain=n_plain),
        grid=(m // tm, (d_plain + d_gate) // tn),
        in_specs=[
            pl.BlockSpec((tm, d), lambda i, j: (i, 0)),
            pl.BlockSpec((1, d), lambda i, j: (0, 0)),
            pl.BlockSpec((1, tn), lambda i, j: (0, jnp.minimum(j, n_plain - 1))),
            pl.BlockSpec((None, d, tn), lambda i, j: (layer, 0, jnp.minimum(j, n_plain - 1))),
            pl.BlockSpec((None, d, tn), lambda i, j: (layer, 0, jnp.maximum(j - n_plain, 0))),
            pl.BlockSpec((None, d, LANES), lambda i, j: (layer, 0, 0)),
        ],
        out_specs=[pl.BlockSpec((tm, tn), lambda i, j: (i, j)),
                   pl.BlockSpec((tm, LANES), lambda i, j: (i, 0))],
        out_shape=[jax.ShapeDtypeStruct((m, d_plain + d_gate), BF16), jax.ShapeDtypeStruct((m, LANES), F32)],
        scratch_shapes=[pltpu.VMEM((tm, d), BF16)],
        compiler_params=_cparams("parallel", "arbitrary"),
        name="in_proj",
    )(x2, g, col_scale, w_in, w_gate, w_f)


def _forget_cum_kernel(hf_ref, b_ref, o_ref, *, blk):
    s = hf_ref.shape[1]
    z = hf_ref[0] + b_ref[...]
    lf = jnp.minimum(z, 0.0) - jnp.log(1.0 + jnp.exp(-jnp.abs(z)))
    lft = lf.T[:SUBLANES]
    row = lax.broadcasted_iota(jnp.int32, (blk, blk), 0)
    col = lax.broadcasted_iota(jnp.int32, (blk, blk), 1)
    upper = (row <= col).astype(BF16)
    carry = jnp.zeros((SUBLANES, 1), F32)
    for i in range(s // blk):
        seg = lft[:, i * blk:(i + 1) * blk]
        p0 = seg.astype(BF16)
        r0 = seg - p0.astype(F32)
        p1 = r0.astype(BF16)
        p2 = (r0 - p1.astype(F32)).astype(BF16)
        cs = _dot(p0, upper) + _dot(p1, upper) + _dot(p2, upper) + carry
        o_ref[0, :, i * blk:(i + 1) * blk] = cs
        carry = cs[:, blk - 1:blk]


def _forget_cum(hf, b_pad):
    b, s, _ = hf.shape
    blk = min(256, s)
    return pl.pallas_call(
        functools.partial(_forget_cum_kernel, blk=blk),
        grid=(b,),
        in_specs=[
            pl.BlockSpec((1, s, LANES), lambda i: (i, 0, 0)),
            pl.BlockSpec((1, LANES), lambda i: (0, 0)),
        ],
        out_specs=pl.BlockSpec((1, SUBLANES, s), lambda i: (i, 0, 0)),
        out_shape=jax.ShapeDtypeStruct((b, SUBLANES, s), F32),
        compiler_params=_cparams("parallel"),
        name="forget_cum",
    )(hf, b_pad)


def _sb_kernel(q_ref, k_ref, v_ref, o_ref, *, tq):
    s = q_ref.shape[1]
    row = lax.broadcasted_iota(jnp.int32, (tq, tq), 0)
    col = lax.broadcasted_iota(jnp.int32, (tq, tq), 1)
    neg_tri = -((row >= col).astype(BF16))
    before = col < row

    def q_block(i):
        q0 = i * tq
        q = q_ref[0, q0:q0 + tq, :]
        zs, sufs = [], []
        for j in range(i + 1):
            z = _dot_nt(q, k_ref[0, j * tq:(j + 1) * tq, :])
            sp = jnp.maximum(z, 0.0) + jnp.log2(1.0 + jnp.exp2(-jnp.abs(z)))
            if j == i:
                sp = jnp.where(before, sp, 0.0)
            zs.append(z)
            yield
            hi = sp.astype(BF16)
            lo = (sp - hi.astype(F32)).astype(BF16)
            sg = _dot(jnp.concatenate([hi, lo], axis=0), neg_tri)
            sufs.append(sg[:tq] + sg[tq:])
            yield
        carry = None
        acc = None
        for j in reversed(range(i + 1)):
            arg = zs[j] + sufs[j]
            if carry is not None:
                arg = arg + carry
            if j == i:
                arg = jnp.where(before, arg, NEG)
            pv = _dot(jnp.exp2(arg).astype(BF16), v_ref[0, j * tq:(j + 1) * tq, :])
            acc = pv if acc is None else acc + pv
            tot = sufs[j][:, 0:1]
            carry = tot if carry is None else carry + tot
            yield
        o_ref[0, q0:q0 + tq, :] = acc.astype(o_ref.dtype)

    _interleave([q_block(i) for i in range(s // tq)], INTERLEAVE_SB)


def _stick_breaking(h3, b, s):
    tq = min(TQ, s)
    blk = lambda off: pl.BlockSpec((1, s, HEAD_DIM), lambda bi, hi: (bi, 0, off + hi))
    return pl.pallas_call(
        functools.partial(_sb_kernel, tq=tq),
        grid=(b, H_SB),
        in_specs=[blk(0), blk(H_SB), blk(2 * H_SB)],
        out_specs=pl.BlockSpec((1, s, HEAD_DIM), lambda bi, hi: (bi, 0, hi)),
        out_shape=jax.ShapeDtypeStruct((b, s, H_SB * HEAD_DIM), BF16),
        compiler_params=_cparams("parallel", "parallel"),
        name="stick_breaking",
    )(h3, h3, h3)


def _halves_rms(x, g2, lo_lane):
    x2 = x * x
    s_lo = jnp.sum(jnp.where(lo_lane, x2, 0.0), axis=-1, keepdims=True)
    s_hi = jnp.sum(jnp.where(lo_lane, 0.0, x2), axis=-1, keepdims=True)
    inv = jnp.where(lo_lane, lax.rsqrt(s_lo / DIFF_DIM + EPS), lax.rsqrt(s_hi / DIFF_DIM + EPS))
    return x * inv * g2


def _diff_kernel(q_ref, k_ref, v_ref, gq_ref, gk_ref, gs_ref, lam_ref, slope_ref, o_ref,
                 kn_ref, vx_ref, *, tq, lambda_init):
    s = q_ref.shape[1]
    lo_lane = lax.broadcasted_iota(jnp.int32, (1, HEAD_DIM), 1) < DIFF_DIM
    row = lax.broadcasted_iota(jnp.int32, (tq, tq), 0)
    col = lax.broadcasted_iota(jnp.int32, (tq, tq), 1)
    visible = (col // CHUNK) <= (row // CHUNK)
    slope = slope_ref[0][:, 0:1] * LOG2E
    diag_bias = (row - jnp.abs(row - col)).astype(F32) * slope
    key_pos = lax.broadcasted_iota(jnp.int32, (1, s), 1).astype(F32) * slope
    lp = lam_ref[...]
    lam = (jnp.exp(jnp.sum(lp[0:1] * lp[1:2], axis=-1, keepdims=True))
           - jnp.exp(jnp.sum(lp[2:3] * lp[3:4], axis=-1, keepdims=True)) + lambda_init)

    kn_ref[...] = _halves_rms(k_ref[0].astype(F32), gk_ref[...], lo_lane).astype(BF16)
    vx_ref[:, :HEAD_DIM] = v_ref[0]
    vx_ref[:, HEAD_DIM:] = jnp.ones((s, HEAD_DIM), BF16)

    def rowmax(x):
        return jnp.max(x, axis=-1, keepdims=True)

    def rowsum(x):
        return jnp.sum(x, axis=-1, keepdims=True)

    def q_block(i):
        q0 = i * tq
        qn = _halves_rms(q_ref[0, q0:q0 + tq, :].astype(F32), gq_ref[...], lo_lane) * (DIFF_DIM ** -0.5 * LOG2E)
        q12 = jnp.concatenate([jnp.where(lo_lane, qn, 0.0), jnp.where(lo_lane, 0.0, qn)], axis=0).astype(BF16)
        bo = key_pos[:, :q0 + tq] - slope * q0
        s1, s2 = [], []
        for j in range(i + 1):
            z = _dot_nt(q12, kn_ref[j * tq:(j + 1) * tq, :])
            if j == i:
                s1.append(jnp.where(visible, z[:tq] + diag_bias, NEG))
                s2.append(jnp.where(visible, z[tq:] + diag_bias, NEG))
            else:
                s1.append(z[:tq] + bo[:, j * tq:(j + 1) * tq])
                s2.append(z[tq:] + bo[:, j * tq:(j + 1) * tq])
            yield
        m1 = rowmax(functools.reduce(jnp.maximum, s1))
        m2 = rowmax(functools.reduce(jnp.maximum, s2))
        acc = None
        for j in range(i + 1):
            p12 = jnp.concatenate([jnp.exp2(s1[j] - m1), jnp.exp2(s2[j] - m2)], axis=0).astype(BF16)
            pv = _dot(p12, vx_ref[j * tq:(j + 1) * tq, :])
            acc = pv if acc is None else acc + pv
            yield
        o = (acc[:tq, :HEAD_DIM] / acc[:tq, HEAD_DIM:]
             - acc[tq:, :HEAD_DIM] * (lam / acc[tq:, HEAD_DIM:]))
        o = _rms_rows(o, gs_ref[...]) * (1.0 - lambda_init)
        o_ref[0, q0:q0 + tq, :] = o.astype(o_ref.dtype)

    _interleave([q_block(i) for i in range(s // tq)], INTERLEAVE_DIFF)


def _diff_attention(h3, b, s, gq2, gk2, gs, lam_params, slopes, lambda_init):
    tq = min(TQ, s)
    off = 3 * H_SB
    blk = lambda o: pl.BlockSpec((1, s, HEAD_DIM), lambda bi, hi: (bi, 0, o + hi))
    vec = pl.BlockSpec((1, HEAD_DIM), lambda bi, hi: (0, 0))
    return pl.pallas_call(
        functools.partial(_diff_kernel, tq=tq, lambda_init=lambda_init),
        grid=(b, H_DIFF),
        in_specs=[blk(off), blk(off + H_DIFF), blk(off + 2 * H_DIFF), vec, vec, vec,
                  pl.BlockSpec((4, DIFF_DIM), lambda bi, hi: (0, 0)),
                  pl.BlockSpec((1, 1, LANES), lambda bi, hi: (hi, 0, 0))],
        out_specs=pl.BlockSpec((1, s, HEAD_DIM), lambda bi, hi: (bi, 0, hi)),
        out_shape=jax.ShapeDtypeStruct((b, s, H_DIFF * HEAD_DIM), BF16),
        scratch_shapes=[pltpu.VMEM((s, HEAD_DIM), BF16), pltpu.VMEM((s, 2 * HEAD_DIM), BF16)],
        compiler_params=_cparams("parallel", "parallel"),
        name="diff_attention",
    )(h3, h3, h3, gq2, gk2, gs, lam_params, slopes)


def _fox_kernel(q_ref, k_ref, v_ref, gq_ref, gk_ref, cum_ref, o_ref, kn_ref, *, tq):
    s = q_ref.shape[1]
    scale = HEAD_DIM ** -0.5
    row = lax.broadcasted_iota(jnp.int32, (tq, tq), 0)
    col = lax.broadcasted_iota(jnp.int32, (tq, tq), 1)
    causal = col <= row

    kn_ref[...] = _rms_rows(k_ref[0].astype(F32), gk_ref[...]).astype(BF16)

    def q_block(i):
        q0 = i * tq
        q = (_rms_rows(q_ref[0, q0:q0 + tq, :].astype(F32), gq_ref[...]) * (scale * LOG2E)).astype(BF16)
        cum = cum_ref[0, 0, :, 0:q0 + tq]
        bias = (cum[:, q0:q0 + 1] - cum) * LOG2E
        ss = []
        for j in range(i + 1):
            sj = _dot_nt(q, kn_ref[j * tq:(j + 1) * tq, :]) + bias[:, j * tq:(j + 1) * tq]
            ss.append(jnp.where(causal, sj, NEG) if j == i else sj)
            yield
        m = jnp.max(functools.reduce(jnp.maximum, ss), axis=-1, keepdims=True)
        ps = []
        for sj in ss:
            ps.append(jnp.exp2(sj - m))
            yield
        l = jnp.sum(functools.reduce(jnp.add, ps), axis=-1, keepdims=True)
        acc = None
        for j, pj in enumerate(ps):
            pv = _dot(pj.astype(BF16), v_ref[0, j * tq:(j + 1) * tq, :])
            acc = pv if acc is None else acc + pv
            yield
        o_ref[0, q0:q0 + tq, :] = (acc / l).astype(o_ref.dtype)

    _interleave([q_block(i) for i in range(s // tq)], INTERLEAVE_FOX)


def _fox_attention(h3, b, s, gq, gk, cum4):
    tq = min(TQ, s)
    off = 3 * H_SB + 3 * H_DIFF
    blk = lambda o: pl.BlockSpec((1, s, HEAD_DIM), lambda bi, hi: (bi, 0, o + hi))
    vec = pl.BlockSpec((1, HEAD_DIM), lambda bi, hi: (0, 0))
    return pl.pallas_call(
        functools.partial(_fox_kernel, tq=tq),
        grid=(b, H_FOX),
        in_specs=[blk(off), blk(off + H_FOX), blk(off + 2 * H_FOX), vec, vec,
                  pl.BlockSpec((1, 1, 1, s), lambda bi, hi: (bi, hi, 0, 0))],
        out_specs=pl.BlockSpec((1, s, HEAD_DIM), lambda bi, hi: (bi, 0, hi)),
        out_shape=jax.ShapeDtypeStruct((b, s, H_FOX * HEAD_DIM), BF16),
        scratch_shapes=[pltpu.VMEM((s, HEAD_DIM), BF16)],
        compiler_params=_cparams("parallel", "parallel"),
        name="fox_attention",
    )(h3, h3, h3, gq, gk, cum4)


def _merge_kernel(oa_ref, ob_ref, oc_ref, ga_ref, gb_ref, gc_ref, wa_ref, wb_ref, wc_ref, y_ref):
    for c in range(0, y_ref.shape[1], PROJ_CHUNK):
        cs = slice(c, c + PROJ_CHUNK)
        y = ga_ref[:, cs].astype(F32) * _dot(oa_ref[...], wa_ref[:, cs])
        y += gb_ref[:, cs].astype(F32) * _dot(ob_ref[...], wb_ref[:, cs])
        y += gc_ref[:, cs].astype(F32) * _dot(oc_ref[...], wc_ref[:, cs])
        y_ref[:, cs] = y.astype(y_ref.dtype)


def _gated_merge(oa, ob, oc, hg, gate_col, wa, wb, wc, layer):
    m = oa.shape[0]
    d = wa.shape[2]
    tm = min(TM_PROJ, m)
    tn = TN_IN_PROJ if d % TN_IN_PROJ == 0 else TN_PROJ
    nb = d // tn
    g0 = gate_col // tn
    o_spec = lambda a: pl.BlockSpec((tm, a.shape[1]), lambda i, j: (i, 0))
    g_spec = lambda br: pl.BlockSpec((tm, tn), lambda i, j: (i, g0 + br * nb + j))
    w_spec = lambda w: pl.BlockSpec((None, w.shape[1], tn), lambda i, j: (layer, 0, j))
    return pl.pallas_call(
        _merge_kernel,
        grid=(m // tm, nb),
        in_specs=[o_spec(oa), o_spec(ob), o_spec(oc), g_spec(0), g_spec(1), g_spec(2),
                  w_spec(wa), w_spec(wb), w_spec(wc)],
        out_specs=pl.BlockSpec((tm, tn), lambda i, j: (i, j)),
        out_shape=jax.ShapeDtypeStruct((m, d), BF16),
        compiler_params=_cparams("parallel", "parallel"),
        name="gated_merge",
    )(oa, ob, oc, hg, hg, hg, wa, wb, wc)


def _resid_proj_kernel(x_ref, y_ref, w_ref, o_ref):
    for c in range(0, o_ref.shape[1], PROJ_CHUNK):
        cs = slice(c, c + PROJ_CHUNK)
        o_ref[:, cs] = x_ref[:, cs] + _dot(y_ref[...], w_ref[:, cs])


def _resid_proj(x2, y, w, layer, in_place):
    m, d = x2.shape
    tm = min(TM_PROJ, m)
    tn = TN_IN_PROJ if d % TN_IN_PROJ == 0 else TN_PROJ
    return pl.pallas_call(
        _resid_proj_kernel,
        grid=(m // tm, d // tn),
        in_specs=[
            pl.BlockSpec((tm, tn), lambda i, j: (i, j)),
            pl.BlockSpec((tm, y.shape[1]), lambda i, j: (i, 0)),
            pl.BlockSpec((None, y.shape[1], tn), lambda i, j: (layer, 0, j)),
        ],
        out_specs=pl.BlockSpec((tm, tn), lambda i, j: (i, j)),
        out_shape=jax.ShapeDtypeStruct((m, d), F32),
        input_output_aliases={0: 0} if in_place else {},
        compiler_params=_cparams("parallel", "parallel"),
        name="resid_proj",
    )(x2, y, w)


def _ffn_kernel(x_ref, g_ref, wg_ref, wu_ref, wd_ref, o_ref, xn_ref):
    @pl.when(pl.program_id(1) == 0)
    def _():
        x = x_ref[...]
        xn_ref[...] = _rms_rows(x, g_ref[...]).astype(BF16)
        o_ref[...] = x

    xn = xn_ref[...]
    acc = None
    for c in range(0, wg_ref.shape[1], PROJ_CHUNK):
        a = _dot(xn, wg_ref[:, c:c + PROJ_CHUNK])
        u = _dot(xn, wu_ref[:, c:c + PROJ_CHUNK])
        part = _dot((_silu(a) * u).astype(BF16), wd_ref[c:c + PROJ_CHUNK, :])
        acc = part if acc is None else acc + part
    o_ref[...] += acc


def _ffn(x2, g, wg, wu, wd, layer):
    m, d = x2.shape
    f = wg.shape[2]
    tm = min(TM_FFN, m)
    tf = min(TF_FFN, f)
    return pl.pallas_call(
        _ffn_kernel,
        grid=(m // tm, f // tf),
        in_specs=[
            pl.BlockSpec((tm, d), lambda i, j: (i, 0)),
            pl.BlockSpec((1, d), lambda i, j: (0, 0)),
            pl.BlockSpec((None, d, tf), lambda i, j: (layer, 0, j)),
            pl.BlockSpec((None, d, tf), lambda i, j: (layer, 0, j)),
            pl.BlockSpec((None, tf, d), lambda i, j: (layer, j, 0)),
        ],
        out_specs=pl.BlockSpec((tm, d), lambda i, j: (i, 0)),
        out_shape=jax.ShapeDtypeStruct((m, d), F32),
        scratch_shapes=[pltpu.VMEM((tm, d), BF16)],
        input_output_aliases={0: 0},
        compiler_params=_cparams("parallel", "arbitrary"),
        name="ffn",
    )(x2, g, wg, wu, wd)


def _row(v):
    return v.astype(F32).reshape(1, -1)


def kernel(x, norm_mix, w_in, b_forget, q_norm_diff, k_norm_diff, lambda_q1, lambda_k1, lambda_q2, lambda_k2, sub_norm_diff, q_norm_fox, k_norm_fox, w_branch_sb, w_branch_diff, w_branch_fox, w_gate, w_out, norm_ffn, w_ff_gate, w_ff_up, w_ff_down):
    b, s, d = x.shape
    depth = w_in.shape[0]
    d_main = HEAD_DIM * 3 * (H_SB + H_DIFF + H_FOX)
    slopes = 2.0 ** (-ALIBI_MAX * jnp.arange(1, H_DIFF + 1, dtype=F32) / H_DIFF)
    slopes = jnp.broadcast_to(slopes[:, None, None], (H_DIFF, 1, LANES))

    w_in_b, w_gate_b, w_out_b = w_in.astype(BF16), w_gate.astype(BF16), w_out.astype(BF16)
    w_f_b = jnp.pad(w_in[:, :, d_main:], ((0, 0), (0, 0), (0, LANES - H_FOX))).astype(BF16)
    w_sb_b, w_diff_b, w_fox_b = w_branch_sb.astype(BF16), w_branch_diff.astype(BF16), w_branch_fox.astype(BF16)
    w_ffg_b, w_ffu_b, w_ffd_b = w_ff_gate.astype(BF16), w_ff_up.astype(BF16), w_ff_down.astype(BF16)
    col_scale = jnp.where(jnp.arange(d_main) < H_SB * HEAD_DIM, HEAD_DIM ** -0.5 * LOG2E, 1.0).astype(F32)[None, :]

    x2 = x.reshape(b * s, d)
    for l in range(depth):
        lambda_init = 0.8 - 0.6 * float(np.exp(-0.3 * l))
        b_pad = jnp.pad(_row(b_forget[l]), ((0, 0), (0, LANES - H_FOX)))

        hg, hf = _in_proj(x2, _row(norm_mix[l]), col_scale, w_in_b, w_gate_b, w_f_b, l)

        cum = _forget_cum(hf.reshape(b, s, LANES), b_pad)
        h3 = hg.reshape(b, s, hg.shape[1])

        o_a = _stick_breaking(h3, b, s)
        lam_params = jnp.stack([lambda_q1[l], lambda_k1[l], lambda_q2[l], lambda_k2[l]]).astype(F32)
        o_b = _diff_attention(h3, b, s,
                              _row(jnp.tile(q_norm_diff[l], 2)), _row(jnp.tile(k_norm_diff[l], 2)),
                              _row(sub_norm_diff[l]), lam_params, slopes, lambda_init)
        o_c = _fox_attention(h3, b, s, _row(q_norm_fox[l]), _row(k_norm_fox[l]),
                             cum.reshape(b, SUBLANES, 1, s))

        y = _gated_merge(o_a.reshape(b * s, -1), o_b.reshape(b * s, -1), o_c.reshape(b * s, -1), hg, d_main,
                         w_sb_b, w_diff_b, w_fox_b, l)
        x2 = _resid_proj(x2, y, w_out_b, l, in_place=l > 0)
        x2 = _ffn(x2, _row(norm_ffn[l]), w_ffg_b, w_ffu_b, w_ffd_b, l)
    return x2.reshape(b, s, d)
```

```python
import functools

import jax
import jax.numpy as jnp
import numpy as np
from jax import lax
from jax.experimental import pallas as pl
from jax.experimental.pallas import tpu as pltpu

F32 = jnp.float32
BF16 = jnp.bfloat16

HEAD_DIM = 128
DIFF_DIM = HEAD_DIM // 2
H_SB = 6
H_DIFF = 5
H_FOX = 5
CHUNK = 64
EPS = 1e-6
ALIBI_MAX = 8.0

LANES = 128
SUBLANES = 8
VMEM_LIMIT_BYTES = 56 * 1024 * 1024

TM_PROJ = 1024
TN_PROJ = 512
TN_IN_PROJ = 1536
TN_WIDE = 1024
PROJ_CHUNK = 256
TM_FFN = 1024
TF_FFN = 512
TQ = 256
TK = 256
INTERLEAVE_SB = 4
INTERLEAVE_DIFF = 3
INTERLEAVE_FOX = 3
NEG = -1e30
LOG2E = float(np.log2(np.e))


def _cparams(*sem):
    return pltpu.CompilerParams(dimension_semantics=sem, vmem_limit_bytes=VMEM_LIMIT_BYTES)


def _rms_rows(x, g):
    ms = jnp.mean(x * x, axis=-1, keepdims=True)
    return x * lax.rsqrt(ms + EPS) * g


def _softplus(z):
    return jnp.maximum(z, 0.0) + jnp.log(1.0 + jnp.exp(-jnp.abs(z)))


def _sigmoid(z):
    return 0.5 * jnp.tanh(0.5 * z) + 0.5


def _silu(z):
    h = 0.5 * z
    return h * jnp.tanh(h) + h


def _interleave(tasks, width):
    tasks = iter(tasks)
    active = [t for _, t in zip(range(width), tasks)]
    while active:
        for t in list(active):
            try:
                next(t)
            except StopIteration:
                active.remove(t)
                nxt = next(tasks, None)
                if nxt is not None:
                    active.append(nxt)


def _dot_nt(a, b):
    return lax.dot_general(a, b, (((1,), (1,)), ((), ())), preferred_element_type=F32)


def _dot(a, b):
    return jnp.dot(a, b, preferred_element_type=F32)


def _in_proj_kernel(x_ref, g_ref, cs_ref, w_ref, wg_ref, wf_ref, o_ref, hf_ref, xn_ref, *, n_plain):
    j = pl.program_id(1)

    @pl.when(j == 0)
    def _():
        xn_ref[...] = _rms_rows(x_ref[...], g_ref[...]).astype(BF16)
        hf_ref[...] = _dot(xn_ref[...], wf_ref[...])

    def project(wt_ref, act):
        for c in range(0, o_ref.shape[1], PROJ_CHUNK):
            cs = slice(c, c + PROJ_CHUNK)
            o_ref[:, cs] = act(_dot(xn_ref[...], wt_ref[:, cs]), cs).astype(o_ref.dtype)

    @pl.when(j < n_plain)
    def _():
        project(w_ref, lambda a, cs: a * cs_ref[:, cs])

    @pl.when(j >= n_plain)
    def _():
        project(wg_ref, lambda a, cs: _sigmoid(a))


def _in_proj(x2, g, col_scale, w_in, w_gate, w_f, layer):
    m, d = x2.shape
    d_plain, d_gate = col_scale.shape[1], w_gate.shape[2]
    tm = min(TM_PROJ, m)
    tn = TN_IN_PROJ if (d_gate % TN_IN_PROJ == 0 and d_plain % TN_IN_PROJ == 0) else TN_PROJ
    n_plain = d_plain // tn
    return pl.pallas_call(
        functools.partial(_in_proj_kernel, n_plain=n_plain),
        grid=(m // tm, (d_plain + d_gate) // tn),
        in_specs=[
            pl.BlockSpec((tm, d), lambda i, j: (i, 0)),
            pl.BlockSpec((1, d), lambda i, j: (0, 0)),
            pl.BlockSpec((1, tn), lambda i, j: (0, jnp.minimum(j, n_plain - 1))),
            pl.BlockSpec((None, d, tn), lambda i, j: (layer, 0, jnp.minimum(j, n_plain - 1))),
            pl.BlockSpec((None, d, tn), lambda i, j: (layer, 0, jnp.maximum(j - n_plain, 0))),
            pl.BlockSpec((None, d, LANES), lambda i, j: (layer, 0, 0)),
        ],
        out_specs=[pl.BlockSpec((tm, tn), lambda i, j: (i, j)),
                   pl.BlockSpec((tm, LANES), lambda i, j: (i, 0))],
        out_shape=[jax.ShapeDtypeStruct((m, d_plain + d_gate), BF16), jax.ShapeDtypeStruct((m, LANES), F32)],
        scratch_shapes=[pltpu.VMEM((tm, d), BF16)],
        compiler_params=_cparams("parallel", "arbitrary"),
        name="in_proj",
    )(x2, g, col_scale, w_in, w_gate, w_f)


def _forget_cum_kernel(hf_ref, b_ref, o_ref, *, blk):
    s = hf_ref.shape[1]
    z = hf_ref[0] + b_ref[...]
    lf = jnp.minimum(z, 0.0) - jnp.log(1.0 + jnp.exp(-jnp.abs(z)))
    lft = lf.T[:SUBLANES]
    row = lax.broadcasted_iota(jnp.int32, (blk, blk), 0)
    col = lax.broadcasted_iota(jnp.int32, (blk, blk), 1)
    upper = (row <= col).astype(BF16)
    carry = jnp.zeros((SUBLANES, 1), F32)
    for i in range(s // blk):
        seg = lft[:, i * blk:(i + 1) * blk]
        p0 = seg.astype(BF16)
        r0 = seg - p0.astype(F32)
        p1 = r0.astype(BF16)
        p2 = (r0 - p1.astype(F32)).astype(BF16)
        cs = _dot(p0, upper) + _dot(p1, upper) + _dot(p2, upper) + carry
        o_ref[0, :, i * blk:(i + 1) * blk] = cs
        carry = cs[:, blk - 1:blk]


def _forget_cum(hf, b_pad):
    b, s, _ = hf.shape
    blk = min(256, s)
    return pl.pallas_call(
        functools.partial(_forget_cum_kernel, blk=blk),
        grid=(b,),
        in_specs=[
            pl.BlockSpec((1, s, LANES), lambda i: (i, 0, 0)),
            pl.BlockSpec((1, LANES), lambda i: (0, 0)),
        ],
        out_specs=pl.BlockSpec((1, SUBLANES, s), lambda i: (i, 0, 0)),
        out_shape=jax.ShapeDtypeStruct((b, SUBLANES, s), F32),
        compiler_params=_cparams("parallel"),
        name="forget_cum",
    )(hf, b_pad)


def _sb_kernel(q_ref, k_ref, v_ref, o_ref, *, tq):
    s = q_ref.shape[1]
    row = lax.broadcasted_iota(jnp.int32, (tq, tq), 0)
    col = lax.broadcasted_iota(jnp.int32, (tq, tq), 1)
    neg_tri = -((row >= col).astype(BF16))
    before = col < row

    def q_block(i):
        q0 = i * tq
        q = q_ref[0, q0:q0 + tq, :]
        zs, sufs = [], []
        for j in range(i + 1):
            z = _dot_nt(q, k_ref[0, j * tq:(j + 1) * tq, :])
            sp = jnp.maximum(z, 0.0) + jnp.log2(1.0 + jnp.exp2(-jnp.abs(z)))
            if j == i:
                sp = jnp.where(before, sp, 0.0)
            zs.append(z)
            yield
            sufs.append(_dot(sp.astype(BF16), neg_tri))
            yield
        carry = None
        acc = None
        for j in reversed(range(i + 1)):
            arg = zs[j] + sufs[j]
            if carry is not None:
                arg = arg + carry
            if j == i:
                arg = jnp.where(before, arg, NEG)
            pv = _dot(jnp.exp2(arg).astype(BF16), v_ref[0, j * tq:(j + 1) * tq, :])
            acc = pv if acc is None else acc + pv
            tot = sufs[j][:, 0:1]
            carry = tot if carry is None else carry + tot
            yield
        o_ref[0, q0:q0 + tq, :] = acc.astype(o_ref.dtype)

    _interleave([q_block(i) for i in range(s // tq)], INTERLEAVE_SB)


def _stick_breaking(h3, b, s):
    tq = min(TQ, s)
    blk = lambda off: pl.BlockSpec((1, s, HEAD_DIM), lambda bi, hi: (bi, 0, off + hi))
    return pl.pallas_call(
        functools.partial(_sb_kernel, tq=tq),
        grid=(b, H_SB),
        in_specs=[blk(0), blk(H_SB), blk(2 * H_SB)],
        out_specs=pl.BlockSpec((1, s, HEAD_DIM), lambda bi, hi: (bi, 0, hi)),
        out_shape=jax.ShapeDtypeStruct((b, s, H_SB * HEAD_DIM), BF16),
        compiler_params=_cparams("parallel", "parallel"),
        name="stick_breaking",
    )(h3, h3, h3)


def _halves_rms(x, g2, lo_lane):
    x2 = x * x
    s_lo = jnp.sum(jnp.where(lo_lane, x2, 0.0), axis=-1, keepdims=True)
    s_hi = jnp.sum(jnp.where(lo_lane, 0.0, x2), axis=-1, keepdims=True)
    inv = jnp.where(lo_lane, lax.rsqrt(s_lo / DIFF_DIM + EPS), lax.rsqrt(s_hi / DIFF_DIM + EPS))
    return x * inv * g2


def _diff_kernel(q_ref, k_ref, v_ref, gq_ref, gk_ref, gs_ref, lam_ref, slope_ref, o_ref,
                 kn_ref, vx_ref, *, tq, lambda_init):
    s = q_ref.shape[1]
    lo_lane = lax.broadcasted_iota(jnp.int32, (1, HEAD_DIM), 1) < DIFF_DIM
    row = lax.broadcasted_iota(jnp.int32, (tq, tq), 0)
    col = lax.broadcasted_iota(jnp.int32, (tq, tq), 1)
    visible = (col // CHUNK) <= (row // CHUNK)
    slope = slope_ref[0][:, 0:1] * LOG2E
    diag_bias = (row - jnp.abs(row - col)).astype(F32) * slope
    key_pos = lax.broadcasted_iota(jnp.int32, (1, s), 1).astype(F32) * slope
    lp = lam_ref[...]
    lam = (jnp.exp(jnp.sum(lp[0:1] * lp[1:2], axis=-1, keepdims=True))
           - jnp.exp(jnp.sum(lp[2:3] * lp[3:4], axis=-1, keepdims=True)) + lambda_init)

    kn_ref[...] = _halves_rms(k_ref[0].astype(F32), gk_ref[...], lo_lane).astype(BF16)
    vx_ref[:, :HEAD_DIM] = v_ref[0]
    vx_ref[:, HEAD_DIM:] = jnp.ones((s, HEAD_DIM), BF16)

    def rowmax(x):
        return jnp.max(x, axis=-1, keepdims=True)

    def rowsum(x):
        return jnp.sum(x, axis=-1, keepdims=True)

    def q_block(i):
        q0 = i * tq
        qn = _halves_rms(q_ref[0, q0:q0 + tq, :].astype(F32), gq_ref[...], lo_lane) * (DIFF_DIM ** -0.5 * LOG2E)
        q12 = jnp.concatenate([jnp.where(lo_lane, qn, 0.0), jnp.where(lo_lane, 0.0, qn)], axis=0).astype(BF16)
        bo = key_pos[:, :q0 + tq] - slope * q0
        s1, s2 = [], []
        for j in range(i + 1):
            z = _dot_nt(q12, kn_ref[j * tq:(j + 1) * tq, :])
            if j == i:
                s1.append(jnp.where(visible, z[:tq] + diag_bias, NEG))
                s2.append(jnp.where(visible, z[tq:] + diag_bias, NEG))
            else:
                s1.append(z[:tq] + bo[:, j * tq:(j + 1) * tq])
                s2.append(z[tq:] + bo[:, j * tq:(j + 1) * tq])
            yield
        m1 = rowmax(functools.reduce(jnp.maximum, s1))
        m2 = rowmax(functools.reduce(jnp.maximum, s2))
        acc = None
        for j in range(i + 1):
            p12 = jnp.concatenate([jnp.exp2(s1[j] - m1), jnp.exp2(s2[j] - m2)], axis=0).astype(BF16)
            pv = _dot(p12, vx_ref[j * tq:(j + 1) * tq, :])
            acc = pv if acc is None else acc + pv
            yield
        o = (acc[:tq, :HEAD_DIM] / acc[:tq, HEAD_DIM:]
             - acc[tq:, :HEAD_DIM] * (lam / acc[tq:, HEAD_DIM:]))
        o = _rms_rows(o, gs_ref[...]) * (1.0 - lambda_init)
        o_ref[0, q0:q0 + tq, :] = o.astype(o_ref.dtype)

    _interleave([q_block(i) for i in range(s // tq)], INTERLEAVE_DIFF)


def _diff_attention(h3, b, s, gq2, gk2, gs, lam_params, slopes, lambda_init):
    tq = min(TQ, s)
    off = 3 * H_SB
    blk = lambda o: pl.BlockSpec((1, s, HEAD_DIM), lambda bi, hi: (bi, 0, o + hi))
    vec = pl.BlockSpec((1, HEAD_DIM), lambda bi, hi: (0, 0))
    return pl.pallas_call(
        functools.partial(_diff_kernel, tq=tq, lambda_init=lambda_init),
        grid=(b, H_DIFF),
        in_specs=[blk(off), blk(off + H_DIFF), blk(off + 2 * H_DIFF), vec, vec, vec,
                  pl.BlockSpec((4, DIFF_DIM), lambda bi, hi: (0, 0)),
                  pl.BlockSpec((1, 1, LANES), lambda bi, hi: (hi, 0, 0))],
        out_specs=pl.BlockSpec((1, s, HEAD_DIM), lambda bi, hi: (bi, 0, hi)),
        out_shape=jax.ShapeDtypeStruct((b, s, H_DIFF * HEAD_DIM), BF16),
        scratch_shapes=[pltpu.VMEM((s, HEAD_DIM), BF16), pltpu.VMEM((s, 2 * HEAD_DIM), BF16)],
        compiler_params=_cparams("parallel", "parallel"),
        name="diff_attention",
    )(h3, h3, h3, gq2, gk2, gs, lam_params, slopes)


def _fox_kernel(q_ref, k_ref, v_ref, gq_ref, gk_ref, cum_ref, o_ref, kn_ref, *, tq):
    s = q_ref.shape[1]
    scale = HEAD_DIM ** -0.5
    row = lax.broadcasted_iota(jnp.int32, (tq, tq), 0)
    col = lax.broadcasted_iota(jnp.int32, (tq, tq), 1)
    causal = col <= row

    kn_ref[...] = _rms_rows(k_ref[0].astype(F32), gk_ref[...]).astype(BF16)

    def q_block(i):
        q0 = i * tq
        q = (_rms_rows(q_ref[0, q0:q0 + tq, :].astype(F32), gq_ref[...]) * (scale * LOG2E)).astype(BF16)
        cum = cum_ref[0, 0, :, 0:q0 + tq]
        bias = (cum[:, q0:q0 + 1] - cum) * LOG2E
        ss = []
        for j in range(i + 1):
            sj = _dot_nt(q, kn_ref[j * tq:(j + 1) * tq, :]) + bias[:, j * tq:(j + 1) * tq]
            ss.append(jnp.where(causal, sj, NEG) if j == i else sj)
            yield
        m = jnp.max(functools.reduce(jnp.maximum, ss), axis=-1, keepdims=True)
        ps = []
        for sj in ss:
            ps.append(jnp.exp2(sj - m))
            yield
        l = jnp.sum(functools.reduce(jnp.add, ps), axis=-1, keepdims=True)
        acc = None
        for j, pj in enumerate(ps):
            pv = _dot(pj.astype(BF16), v_ref[0, j * tq:(j + 1) * tq, :])
            acc = pv if acc is None else acc + pv
            yield
        o_ref[0, q0:q0 + tq, :] = (acc / l).astype(o_ref.dtype)

    _interleave([q_block(i) for i in range(s // tq)], INTERLEAVE_FOX)


def _fox_attention(h3, b, s, gq, gk, cum4):
    tq = min(TQ, s)
    off = 3 * H_SB + 3 * H_DIFF
    blk = lambda o: pl.BlockSpec((1, s, HEAD_DIM), lambda bi, hi: (bi, 0, o + hi))
    vec = pl.BlockSpec((1, HEAD_DIM), lambda bi, hi: (0, 0))
    return pl.pallas_call(
        functools.partial(_fox_kernel, tq=tq),
        grid=(b, H_FOX),
        in_specs=[blk(off), blk(off + H_FOX), blk(off + 2 * H_FOX), vec, vec,
                  pl.BlockSpec((1, 1, 1, s), lambda bi, hi: (bi, hi, 0, 0))],
        out_specs=pl.BlockSpec((1, s, HEAD_DIM), lambda bi, hi: (bi, 0, hi)),
        out_shape=jax.ShapeDtypeStruct((b, s, H_FOX * HEAD_DIM), BF16),
        scratch_shapes=[pltpu.VMEM((s, HEAD_DIM), BF16)],
        compiler_params=_cparams("parallel", "parallel"),
        name="fox_attention",
    )(h3, h3, h3, gq, gk, cum4)


def _merge_kernel(oa_ref, ob_ref, oc_ref, ga_ref, gb_ref, gc_ref, wa_ref, wb_ref, wc_ref, y_ref):
    for c in range(0, y_ref.shape[1], PROJ_CHUNK):
        cs = slice(c, c + PROJ_CHUNK)
        y = ga_ref[:, cs].astype(F32) * _dot(oa_ref[...], wa_ref[:, cs])
        y += gb_ref[:, cs].astype(F32) * _dot(ob_ref[...], wb_ref[:, cs])
        y += gc_ref[:, cs].astype(F32) * _dot(oc_ref[...], wc_ref[:, cs])
        y_ref[:, cs] = y.astype(y_ref.dtype)


def _gated_merge(oa, ob, oc, hg, gate_col, wa, wb, wc, layer):
    m = oa.shape[0]
    d = wa.shape[2]
    tm = min(TM_PROJ, m)
    tn = TN_WIDE if d % TN_WIDE == 0 else TN_PROJ
    nb = d // tn
    g0 = gate_col // tn
    o_spec = lambda a: pl.BlockSpec((tm, a.shape[1]), lambda i, j: (i, 0))
    g_spec = lambda br: pl.BlockSpec((tm, tn), lambda i, j: (i, g0 + br * nb + j))
    w_spec = lambda w: pl.BlockSpec((None, w.shape[1], tn), lambda i, j: (layer, 0, j))
    return pl.pallas_call(
        _merge_kernel,
        grid=(m // tm, nb),
        in_specs=[o_spec(oa), o_spec(ob), o_spec(oc), g_spec(0), g_spec(1), g_spec(2),
                  w_spec(wa), w_spec(wb), w_spec(wc)],
        out_specs=pl.BlockSpec((tm, tn), lambda i, j: (i, j)),
        out_shape=jax.ShapeDtypeStruct((m, d), BF16),
        compiler_params=_cparams("parallel", "parallel"),
        name="gated_merge",
    )(oa, ob, oc, hg, hg, hg, wa, wb, wc)


def _resid_proj_kernel(x_ref, y_ref, w_ref, o_ref):
    for c in range(0, o_ref.shape[1], PROJ_CHUNK):
        cs = slice(c, c + PROJ_CHUNK)
        o_ref[:, cs] = x_ref[:, cs] + _dot(y_ref[...], w_ref[:, cs])


def _resid_proj(x2, y, w, layer, in_place):
    m, d = x2.shape
    tm = min(TM_PROJ, m)
    tn = TN_WIDE if d % TN_WIDE == 0 else TN_PROJ
    return pl.pallas_call(
        _resid_proj_kernel,
        grid=(m // tm, d // tn),
        in_specs=[
            pl.BlockSpec((tm, tn), lambda i, j: (i, j)),
            pl.BlockSpec((tm, y.shape[1]), lambda i, j: (i, 0)),
            pl.BlockSpec((None, y.shape[1], tn), lambda i, j: (layer, 0, j)),
        ],
        out_specs=pl.BlockSpec((tm, tn), lambda i, j: (i, j)),
        out_shape=jax.ShapeDtypeStruct((m, d), F32),
        input_output_aliases={0: 0} if in_place else {},
        compiler_params=_cparams("parallel", "parallel"),
        name="resid_proj",
    )(x2, y, w)


def _ffn_kernel(x_ref, g_ref, wg_ref, wu_ref, wd_ref, o_ref, xn_ref):
    @pl.when(pl.program_id(1) == 0)
    def _():
        x = x_ref[...]
        xn_ref[...] = _rms_rows(x, g_ref[...]).astype(BF16)
        o_ref[...] = x

    xn = xn_ref[...]
    acc = None
    for c in range(0, wg_ref.shape[1], PROJ_CHUNK):
        a = _dot(xn, wg_ref[:, c:c + PROJ_CHUNK])
        u = _dot(xn, wu_ref[:, c:c + PROJ_CHUNK])
        part = _dot((_silu(a) * u).astype(BF16), wd_ref[c:c + PROJ_CHUNK, :])
        acc = part if acc is None else acc + part
    o_ref[...] += acc


def _ffn(x2, g, wg, wu, wd, layer):
    m, d = x2.shape
    f = wg.shape[2]
    tm = min(TM_FFN, m)
    tf = min(TF_FFN, f)
    return pl.pallas_call(
        _ffn_kernel,
        grid=(m // tm, f // tf),
        in_specs=[
            pl.BlockSpec((tm, d), lambda i, j: (i, 0)),
            pl.BlockSpec((1, d), lambda i, j: (0, 0)),
            pl.BlockSpec((None, d, tf), lambda i, j: (layer, 0, j)),
            pl.BlockSpec((None, d, tf), lambda i, j: (layer, 0, j)),
            pl.BlockSpec((None, tf, d), lambda i, j: (layer, j, 0)),
        ],
        out_specs=pl.BlockSpec((tm, d), lambda i, j: (i, 0)),
        out_shape=jax.ShapeDtypeStruct((m, d), F32),
        scratch_shapes=[pltpu.VMEM((tm, d), BF16)],
        input_output_aliases={0: 0},
        compiler_params=_cparams("parallel", "arbitrary"),
        name="ffn",
    )(x2, g, wg, wu, wd)


def _row(v):
    return v.astype(F32).reshape(1, -1)


def kernel(x, norm_mix, w_in, b_forget, q_norm_diff, k_norm_diff, lambda_q1, lambda_k1, lambda_q2, lambda_k2, sub_norm_diff, q_norm_fox, k_norm_fox, w_branch_sb, w_branch_diff, w_branch_fox, w_gate, w_out, norm_ffn, w_ff_gate, w_ff_up, w_ff_down):
    b, s, d = x.shape
    depth = w_in.shape[0]
    d_main = HEAD_DIM * 3 * (H_SB + H_DIFF + H_FOX)
    slopes = 2.0 ** (-ALIBI_MAX * jnp.arange(1, H_DIFF + 1, dtype=F32) / H_DIFF)
    slopes = jnp.broadcast_to(slopes[:, None, None], (H_DIFF, 1, LANES))

    w_in_b, w_gate_b, w_out_b = w_in.astype(BF16), w_gate.astype(BF16), w_out.astype(BF16)
    w_f_b = jnp.pad(w_in[:, :, d_main:], ((0, 0), (0, 0), (0, LANES - H_FOX))).astype(BF16)
    w_sb_b, w_diff_b, w_fox_b = w_branch_sb.astype(BF16), w_branch_diff.astype(BF16), w_branch_fox.astype(BF16)
    w_ffg_b, w_ffu_b, w_ffd_b = w_ff_gate.astype(BF16), w_ff_up.astype(BF16), w_ff_down.astype(BF16)
    col_scale = jnp.where(jnp.arange(d_main) < H_SB * HEAD_DIM, HEAD_DIM ** -0.5 * LOG2E, 1.0).astype(F32)[None, :]

    x2 = x.reshape(b * s, d)
    for l in range(depth):
        lambda_init = 0.8 - 0.6 * float(np.exp(-0.3 * l))
        b_pad = jnp.pad(_row(b_forget[l]), ((0, 0), (0, LANES - H_FOX)))

        hg, hf = _in_proj(x2, _row(norm_mix[l]), col_scale, w_in_b, w_gate_b, w_f_b, l)

        cum = _forget_cum(hf.reshape(b, s, LANES), b_pad)
        h3 = hg.reshape(b, s, hg.shape[1])

        o_a = _stick_breaking(h3, b, s)
        lam_params = jnp.stack([lambda_q1[l], lambda_k1[l], lambda_q2[l], lambda_k2[l]]).astype(F32)
        o_b = _diff_attention(h3, b, s,
                              _row(jnp.tile(q_norm_diff[l], 2)), _row(jnp.tile(k_norm_diff[l], 2)),
                              _row(sub_norm_diff[l]), lam_params, slopes, lambda_init)
        o_c = _fox_attention(h3, b, s, _row(q_norm_fox[l]), _row(k_norm_fox[l]),
                             cum.reshape(b, SUBLANES, 1, s))

        y = _gated_merge(o_a.reshape(b * s, -1), o_b.reshape(b * s, -1), o_c.reshape(b * s, -1), hg, d_main,
                         w_sb_b, w_diff_b, w_fox_b, l)
        x2 = _resid_proj(x2, y, w_out_b, l, in_place=l > 0)
        x2 = _ffn(x2, _row(norm_ffn[l]), w_ffg_b, w_ffu_b, w_ffd_b, l)
    return x2.reshape(b, s, d)
```

```python
import functools

import jax
import jax.numpy as jnp
import numpy as np
from jax import lax
from jax.experimental import pallas as pl
from jax.experimental.pallas import tpu as pltpu

F32 = jnp.float32
BF16 = jnp.bfloat16

HEAD_DIM = 128
DIFF_DIM = HEAD_DIM // 2
H_SB = 6
H_DIFF = 5
H_FOX = 5
CHUNK = 64
EPS = 1e-6
ALIBI_MAX = 8.0

LANES = 128
SUBLANES = 8
VMEM_LIMIT_BYTES = 56 * 1024 * 1024

TM_PROJ = 1024
TN_PROJ = 512
TN_IN_PROJ = 1536
TN_WIDE = 1024
PROJ_CHUNK = 256
TM_FFN = 1024
TF_FFN = 512
TQ = 256
TK = 256
INTERLEAVE_SB = 4
INTERLEAVE_DIFF = 3
INTERLEAVE_FOX = 3
INTERLEAVE_BOUNDED = 2
NEG = -1e30
MAX_LOGIT_BOUND = 48.0
LOG2E = float(np.log2(np.e))


def _cparams(*sem):
    return pltpu.CompilerParams(dimension_semantics=sem, vmem_limit_bytes=VMEM_LIMIT_BYTES)


def _rms_rows(x, g):
    ms = jnp.mean(x * x, axis=-1, keepdims=True)
    return x * lax.rsqrt(ms + EPS) * g


def _softplus(z):
    return jnp.maximum(z, 0.0) + jnp.log(1.0 + jnp.exp(-jnp.abs(z)))


def _sigmoid(z):
    return 0.5 * jnp.tanh(0.5 * z) + 0.5


def _silu(z):
    h = 0.5 * z
    return h * jnp.tanh(h) + h


def _interleave(tasks, width):
    tasks = iter(tasks)
    active = [t for _, t in zip(range(width), tasks)]
    while active:
        for t in list(active):
            try:
                next(t)
            except StopIteration:
                active.remove(t)
                nxt = next(tasks, None)
                if nxt is not None:
                    active.append(nxt)


def _dot_nt(a, b):
    return lax.dot_general(a, b, (((1,), (1,)), ((), ())), preferred_element_type=F32)


def _dot(a, b):
    return jnp.dot(a, b, preferred_element_type=F32)


def _in_proj_kernel(x_ref, g_ref, cs_ref, w_ref, wg_ref, wf_ref, o_ref, hf_ref, xn_ref, *, n_plain):
    j = pl.program_id(1)

    @pl.when(j == 0)
    def _():
        xn_ref[...] = _rms_rows(x_ref[...], g_ref[...]).astype(BF16)
        hf_ref[...] = _dot(xn_ref[...], wf_ref[...])

    def project(wt_ref, act):
        for c in range(0, o_ref.shape[1], PROJ_CHUNK):
            cs = slice(c, c + PROJ_CHUNK)
            o_ref[:, cs] = act(_dot(xn_ref[...], wt_ref[:, cs]), cs).astype(o_ref.dtype)

    @pl.when(j < n_plain)
    def _():
        project(w_ref, lambda a, cs: a * cs_ref[:, cs])

    @pl.when(j >= n_plain)
    def _():
        project(wg_ref, lambda a, cs: _sigmoid(a))


def _in_proj(x2, g, col_scale, w_in, w_gate, w_f, layer):
    m, d = x2.shape
    d_plain, d_gate = col_scale.shape[1], w_gate.shape[2]
    tm = min(TM_PROJ, m)
    tn = TN_IN_PROJ if (d_gate % TN_IN_PROJ == 0 and d_plain % TN_IN_PROJ == 0) else TN_PROJ
    n_plain = d_plain // tn
    return pl.pallas_call(
        functools.partial(_in_proj_kernel, n_plain=n_plain),
        grid=(m // tm, (d_plain + d_gate) // tn),
        in_specs=[
            pl.BlockSpec((tm, d), lambda i, j: (i, 0)),
            pl.BlockSpec((1, d), lambda i, j: (0, 0)),
            pl.BlockSpec((1, tn), lambda i, j: (0, jnp.minimum(j, n_plain - 1))),
            pl.BlockSpec((None, d, tn), lambda i, j: (layer, 0, jnp.minimum(j, n_plain - 1))),
            pl.BlockSpec((None, d, tn), lambda i, j: (layer, 0, jnp.maximum(j - n_plain, 0))),
            pl.BlockSpec((None, d, LANES), lambda i, j: (layer, 0, 0)),
        ],
        out_specs=[pl.BlockSpec((tm, tn), lambda i, j: (i, j)),
                   pl.BlockSpec((tm, LANES), lambda i, j: (i, 0))],
        out_shape=[jax.ShapeDtypeStruct((m, d_plain + d_gate), BF16), jax.ShapeDtypeStruct((m, LANES), F32)],
        scratch_shapes=[pltpu.VMEM((tm, d), BF16)],
        compiler_params=_cparams("parallel", "arbitrary"),
        name="in_proj",
    )(x2, g, col_scale, w_in, w_gate, w_f)


def _split3(x):
    p0 = x.astype(BF16)
    r0 = x - p0.astype(F32)
    p1 = r0.astype(BF16)
    return p0, p1, (r0 - p1.astype(F32)).astype(BF16)


def _forget_cum_kernel(hf_ref, b_ref, o_ref, c_ref, *, blk):
    s = hf_ref.shape[1]
    z = hf_ref[0] + b_ref[...]
    lf = jnp.minimum(z, 0.0) - jnp.log(1.0 + jnp.exp(-jnp.abs(z)))
    lft = lf.T[:SUBLANES]
    row = lax.broadcasted_iota(jnp.int32, (blk, blk), 0)
    col = lax.broadcasted_iota(jnp.int32, (blk, blk), 1)
    upper = (row <= col).astype(BF16)
    lower = (row >= col).astype(BF16)
    carry = jnp.zeros((SUBLANES, 1), F32)
    carry_c = jnp.zeros((1, LANES), F32)
    for i in range(s // blk):
        p0, p1, p2 = _split3(lft[:, i * blk:(i + 1) * blk])
        cs = _dot(p0, upper) + _dot(p1, upper) + _dot(p2, upper) + carry
        o_ref[0, :, i * blk:(i + 1) * blk] = cs
        carry = cs[:, blk - 1:blk]
        p0, p1, p2 = _split3(lf[i * blk:(i + 1) * blk, :])
        cc = _dot(lower, p0) + _dot(lower, p1) + _dot(lower, p2) + carry_c
        c_ref[0, i * blk:(i + 1) * blk, :] = cc
        carry_c = cc[blk - 1:blk, :]


def _forget_cum(hf, b_pad):
    b, s, _ = hf.shape
    blk = min(256, s)
    return pl.pallas_call(
        functools.partial(_forget_cum_kernel, blk=blk),
        grid=(b,),
        in_specs=[
            pl.BlockSpec((1, s, LANES), lambda i: (i, 0, 0)),
            pl.BlockSpec((1, LANES), lambda i: (0, 0)),
        ],
        out_specs=[pl.BlockSpec((1, SUBLANES, s), lambda i: (i, 0, 0)),
                   pl.BlockSpec((1, s, LANES), lambda i: (i, 0, 0))],
        out_shape=[jax.ShapeDtypeStruct((b, SUBLANES, s), F32), jax.ShapeDtypeStruct((b, s, LANES), F32)],
        compiler_params=_cparams("parallel"),
        name="forget_cum",
    )(hf, b_pad)


def _sb_kernel(q_ref, k_ref, v_ref, o_ref, *, tq):
    s = q_ref.shape[1]
    row = lax.broadcasted_iota(jnp.int32, (tq, tq), 0)
    col = lax.broadcasted_iota(jnp.int32, (tq, tq), 1)
    neg_tri = -((row >= col).astype(BF16))
    before = col < row

    def q_block(i):
        q0 = i * tq
        q = q_ref[0, q0:q0 + tq, :]
        zs, sufs = [], []
        for j in range(i + 1):
            z = _dot_nt(q, k_ref[0, j * tq:(j + 1) * tq, :])
            sp = jnp.maximum(z, 0.0) + jnp.log2(1.0 + jnp.exp2(-jnp.abs(z)))
            if j == i:
                sp = jnp.where(before, sp, 0.0)
            zs.append(z)
            yield
            sufs.append(_dot(sp.astype(BF16), neg_tri))
            yield
        carry = None
        acc = None
        for j in reversed(range(i + 1)):
            arg = zs[j] + sufs[j]
            if carry is not None:
                arg = arg + carry
            if j == i:
                arg = jnp.where(before, arg, NEG)
            pv = _dot(jnp.exp2(arg).astype(BF16), v_ref[0, j * tq:(j + 1) * tq, :])
            acc = pv if acc is None else acc + pv
            tot = sufs[j][:, 0:1]
            carry = tot if carry is None else carry + tot
            yield
        o_ref[0, q0:q0 + tq, :] = acc.astype(o_ref.dtype)

    _interleave([q_block(i) for i in range(s // tq)], INTERLEAVE_SB)


def _stick_breaking(h3, b, s):
    tq = min(TQ, s)
    blk = lambda off: pl.BlockSpec((1, s, HEAD_DIM), lambda bi, hi: (bi, 0, off + hi))
    return pl.pallas_call(
        functools.partial(_sb_kernel, tq=tq),
        grid=(b, H_SB),
        in_specs=[blk(0), blk(H_SB), blk(2 * H_SB)],
        out_specs=pl.BlockSpec((1, s, HEAD_DIM), lambda bi, hi: (bi, 0, hi)),
        out_shape=jax.ShapeDtypeStruct((b, s, H_SB * HEAD_DIM), BF16),
        compiler_params=_cparams("parallel", "parallel"),
        name="stick_breaking",
    )(h3, h3, h3)


def _halves_rms(x, g2, lo_lane):
    x2 = x * x
    s_lo = jnp.sum(jnp.where(lo_lane, x2, 0.0), axis=-1, keepdims=True)
    s_hi = jnp.sum(jnp.where(lo_lane, 0.0, x2), axis=-1, keepdims=True)
    inv = jnp.where(lo_lane, lax.rsqrt(s_lo / DIFF_DIM + EPS), lax.rsqrt(s_hi / DIFF_DIM + EPS))
    return x * inv * g2


def _diff_kernel(q_ref, k_ref, v_ref, gq_ref, gk_ref, gs_ref, lam_ref, slope_ref, mb_ref, o_ref,
                 kn_ref, vx_ref, *bias_ref, tq, lambda_init, bounded):
    s = q_ref.shape[1]
    lo_lane = lax.broadcasted_iota(jnp.int32, (1, HEAD_DIM), 1) < DIFF_DIM
    row = lax.broadcasted_iota(jnp.int32, (tq, tq), 0)
    col = lax.broadcasted_iota(jnp.int32, (tq, tq), 1)
    visible = (col // CHUNK) <= (row // CHUNK)
    slope = slope_ref[0][:, 0:1] * LOG2E
    diag_bias = (row - jnp.abs(row - col)).astype(F32) * slope
    key_pos = lax.broadcasted_iota(jnp.int32, (1, s), 1).astype(F32) * slope
    lp = lam_ref[...]
    lam = (jnp.exp(jnp.sum(lp[0:1] * lp[1:2], axis=-1, keepdims=True))
           - jnp.exp(jnp.sum(lp[2:3] * lp[3:4], axis=-1, keepdims=True)) + lambda_init)

    kn_ref[...] = _halves_rms(k_ref[0].astype(F32), gk_ref[...], lo_lane).astype(BF16)
    vx_ref[:, :HEAD_DIM] = v_ref[0]
    vx_ref[:, HEAD_DIM:] = jnp.ones((s, HEAD_DIM), BF16)

    def rowmax(x):
        return jnp.max(x, axis=-1, keepdims=True)

    def rowsum(x):
        return jnp.sum(x, axis=-1, keepdims=True)

    def q_block(i):
        q0 = i * tq
        qn = _halves_rms(q_ref[0, q0:q0 + tq, :].astype(F32), gq_ref[...], lo_lane) * (DIFF_DIM ** -0.5 * LOG2E)
        q12 = jnp.concatenate([jnp.where(lo_lane, qn, 0.0), jnp.where(lo_lane, 0.0, qn)], axis=0).astype(BF16)
        bo = key_pos[:, :q0 + tq] - slope * q0
        s1, s2 = [], []
        for j in range(i + 1):
            z = _dot_nt(q12, kn_ref[j * tq:(j + 1) * tq, :])
            if j == i:
                s1.append(jnp.where(visible, z[:tq] + diag_bias, NEG))
                s2.append(jnp.where(visible, z[tq:] + diag_bias, NEG))
            else:
                s1.append(z[:tq] + bo[:, j * tq:(j + 1) * tq])
                s2.append(z[tq:] + bo[:, j * tq:(j + 1) * tq])
            yield
        m1 = rowmax(functools.reduce(jnp.maximum, s1))
        m2 = rowmax(functools.reduce(jnp.maximum, s2))
        acc = None
        for j in range(i + 1):
            p12 = jnp.concatenate([jnp.exp2(s1[j] - m1), jnp.exp2(s2[j] - m2)], axis=0).astype(BF16)
            pv = _dot(p12, vx_ref[j * tq:(j + 1) * tq, :])
            acc = pv if acc is None else acc + pv
            yield
        finish(q0, acc)

    def finish(q0, acc):
        o = (acc[:tq, :HEAD_DIM] / acc[:tq, HEAD_DIM:]
             - acc[tq:, :HEAD_DIM] * (lam / acc[tq:, HEAD_DIM:]))
        o = _rms_rows(o, gs_ref[...]) * (1.0 - lambda_init)
        o_ref[0, q0:q0 + tq, :] = o.astype(o_ref.dtype)

    def q_block_bounded(i):
        q0 = i * tq
        qn = _halves_rms(q_ref[0, q0:q0 + tq, :].astype(F32), gq_ref[...], lo_lane) * (DIFF_DIM ** -0.5 * LOG2E)
        q12 = jnp.concatenate([jnp.where(lo_lane, qn, 0.0), jnp.where(lo_lane, 0.0, qn)], axis=0).astype(BF16)
        acc = None
        for j in range(i + 1):
            z = _dot_nt(q12, kn_ref[j * tq:(j + 1) * tq, :])
            bias = bias_ref[0][i - j]
            p12 = jnp.exp2(jnp.concatenate([z[:tq] + bias, z[tq:] + bias], axis=0)).astype(BF16)
            pv = _dot(p12, vx_ref[j * tq:(j + 1) * tq, :])
            acc = pv if acc is None else acc + pv
            yield
        finish(q0, acc)

    if bounded:
        dist = (row - col).astype(F32)
        mb = mb_ref[...]
        bias_ref[0][0] = jnp.where(visible, -slope * jnp.abs(dist) - mb, NEG)
        for dlt in range(1, s // tq):
            bias_ref[0][dlt] = -slope * (dist + float(dlt * tq)) - mb
        _interleave([q_block_bounded(i) for i in range(s // tq)], INTERLEAVE_BOUNDED)
    else:
        _interleave([q_block(i) for i in range(s // tq)], INTERLEAVE_DIFF)


def _logit_bound(gq, gk, dim):
    return (jnp.max(jnp.abs(gq)) * jnp.max(jnp.abs(gk)) * (dim ** 0.5 * LOG2E * 1.02)).astype(F32).reshape(1, 1)


def _diff_attention(h3, b, s, gq2, gk2, gs, lam_params, slopes, lambda_init):
    tq = min(TQ, s)
    off = 3 * H_SB
    blk = lambda o: pl.BlockSpec((1, s, HEAD_DIM), lambda bi, hi: (bi, 0, o + hi))
    vec = pl.BlockSpec((1, HEAD_DIM), lambda bi, hi: (0, 0))
    one = pl.BlockSpec((1, 1), lambda bi, hi: (0, 0))
    mb = _logit_bound(gq2, gk2, DIFF_DIM)

    def call(bounded):
        bias = [pltpu.VMEM((s // tq, tq, tq), F32)] if bounded else []
        return pl.pallas_call(
            functools.partial(_diff_kernel, tq=tq, lambda_init=lambda_init, bounded=bounded),
            grid=(b, H_DIFF),
            in_specs=[blk(off), blk(off + H_DIFF), blk(off + 2 * H_DIFF), vec, vec, vec,
                      pl.BlockSpec((4, DIFF_DIM), lambda bi, hi: (0, 0)),
                      pl.BlockSpec((1, 1, LANES), lambda bi, hi: (hi, 0, 0)), one],
            out_specs=pl.BlockSpec((1, s, HEAD_DIM), lambda bi, hi: (bi, 0, hi)),
            out_shape=jax.ShapeDtypeStruct((b, s, H_DIFF * HEAD_DIM), BF16),
            scratch_shapes=[pltpu.VMEM((s, HEAD_DIM), BF16), pltpu.VMEM((s, 2 * HEAD_DIM), BF16)] + bias,
            compiler_params=_cparams("parallel", "parallel"),
            name="diff_attention_bounded" if bounded else "diff_attention",
        )

    return lax.cond(mb[0, 0] < MAX_LOGIT_BOUND, call(True), call(False),
                    h3, h3, h3, gq2, gk2, gs, lam_params, slopes, mb)


def _fox_kernel(q_ref, k_ref, v_ref, gq_ref, gk_ref, cum_ref, cumc_ref, mb_ref, o_ref, kn_ref, *vx_ref,
                tq, bounded):
    s = q_ref.shape[1]
    scale = HEAD_DIM ** -0.5
    row = lax.broadcasted_iota(jnp.int32, (tq, tq), 0)
    col = lax.broadcasted_iota(jnp.int32, (tq, tq), 1)
    causal = col <= row

    kn_ref[...] = _rms_rows(k_ref[0].astype(F32), gk_ref[...]).astype(BF16)

    def q_block(i):
        q0 = i * tq
        q = (_rms_rows(q_ref[0, q0:q0 + tq, :].astype(F32), gq_ref[...]) * (scale * LOG2E)).astype(BF16)
        cum = cum_ref[0, 0, :, 0:q0 + tq]
        bias = (cum[:, q0:q0 + 1] - cum) * LOG2E
        ss = []
        for j in range(i + 1):
            sj = _dot_nt(q, kn_ref[j * tq:(j + 1) * tq, :]) + bias[:, j * tq:(j + 1) * tq]
            ss.append(jnp.where(causal, sj, NEG) if j == i else sj)
            yield
        m = jnp.max(functools.reduce(jnp.maximum, ss), axis=-1, keepdims=True)
        ps = []
        for sj in ss:
            ps.append(jnp.exp2(sj - m))
            yield
        l = jnp.sum(functools.reduce(jnp.add, ps), axis=-1, keepdims=True)
        acc = None
        for j, pj in enumerate(ps):
            pv = _dot(pj.astype(BF16), v_ref[0, j * tq:(j + 1) * tq, :])
            acc = pv if acc is None else acc + pv
            yield
        o_ref[0, q0:q0 + tq, :] = (acc / l).astype(o_ref.dtype)

    def q_block_bounded(i):
        q0 = i * tq
        q = (_rms_rows(q_ref[0, q0:q0 + tq, :].astype(F32), gq_ref[...]) * (scale * LOG2E)).astype(BF16)
        head = lax.broadcasted_iota(jnp.int32, (1, LANES), 1) == pl.program_id(1)
        f_t = jnp.sum(jnp.where(head, cumc_ref[0, q0:q0 + tq, :], 0.0), axis=-1, keepdims=True)
        row_term = f_t * LOG2E - mb_ref[...]
        col_term = cum_ref[0, 0, :, 0:q0 + tq] * LOG2E
        acc = None
        for j in range(i + 1):
            sj = _dot_nt(q, kn_ref[j * tq:(j + 1) * tq, :]) + row_term - col_term[:, j * tq:(j + 1) * tq]
            if j == i:
                sj = jnp.where(causal, sj, NEG)
            pv = _dot(jnp.exp2(sj).astype(BF16), vx_ref[0][j * tq:(j + 1) * tq, :])
            acc = pv if acc is None else acc + pv
            yield
        o_ref[0, q0:q0 + tq, :] = (acc[:, :HEAD_DIM] / acc[:, HEAD_DIM:]).astype(o_ref.dtype)

    if bounded:
        vx_ref[0][:, :HEAD_DIM] = v_ref[0]
        vx_ref[0][:, HEAD_DIM:] = jnp.ones((s, HEAD_DIM), BF16)
        _interleave([q_block_bounded(i) for i in range(s // tq)], INTERLEAVE_BOUNDED)
    else:
        _interleave([q_block(i) for i in range(s // tq)], INTERLEAVE_FOX)


def _fox_attention(h3, b, s, gq, gk, cum4, cum_col):
    tq = min(TQ, s)
    off = 3 * H_SB + 3 * H_DIFF
    blk = lambda o: pl.BlockSpec((1, s, HEAD_DIM), lambda bi, hi: (bi, 0, o + hi))
    vec = pl.BlockSpec((1, HEAD_DIM), lambda bi, hi: (0, 0))
    mb = _logit_bound(gq, gk, HEAD_DIM)

    def call(bounded):
        vx = [pltpu.VMEM((s, 2 * HEAD_DIM), BF16)] if bounded else []
        return pl.pallas_call(
            functools.partial(_fox_kernel, tq=tq, bounded=bounded),
            grid=(b, H_FOX),
            in_specs=[blk(off), blk(off + H_FOX), blk(off + 2 * H_FOX), vec, vec,
                      pl.BlockSpec((1, 1, 1, s), lambda bi, hi: (bi, hi, 0, 0)),
                      pl.BlockSpec((1, s, LANES), lambda bi, hi: (bi, 0, 0)),
                      pl.BlockSpec((1, 1), lambda bi, hi: (0, 0))],
            out_specs=pl.BlockSpec((1, s, HEAD_DIM), lambda bi, hi: (bi, 0, hi)),
            out_shape=jax.ShapeDtypeStruct((b, s, H_FOX * HEAD_DIM), BF16),
            scratch_shapes=[pltpu.VMEM((s, HEAD_DIM), BF16)] + vx,
            compiler_params=_cparams("parallel", "parallel"),
            name="fox_attention_bounded" if bounded else "fox_attention",
        )

    return lax.cond(mb[0, 0] < MAX_LOGIT_BOUND, call(True), call(False), h3, h3, h3, gq, gk, cum4, cum_col, mb)


def _merge_kernel(oa_ref, ob_ref, oc_ref, ga_ref, gb_ref, gc_ref, wa_ref, wb_ref, wc_ref, y_ref):
    for c in range(0, y_ref.shape[1], PROJ_CHUNK):
        cs = slice(c, c + PROJ_CHUNK)
        y = ga_ref[:, cs].astype(F32) * _dot(oa_ref[...], wa_ref[:, cs])
        y += gb_ref[:, cs].astype(F32) * _dot(ob_ref[...], wb_ref[:, cs])
        y += gc_ref[:, cs].astype(F32) * _dot(oc_ref[...], wc_ref[:, cs])
        y_ref[:, cs] = y.astype(y_ref.dtype)


def _gated_merge(oa, ob, oc, hg, gate_col, wa, wb, wc, layer):
    m = oa.shape[0]
    d = wa.shape[2]
    tm = min(TM_PROJ, m)
    tn = TN_WIDE if d % TN_WIDE == 0 else TN_PROJ
    nb = d // tn
    g0 = gate_col // tn
    o_spec = lambda a: pl.BlockSpec((tm, a.shape[1]), lambda i, j: (i, 0))
    g_spec = lambda br: pl.BlockSpec((tm, tn), lambda i, j: (i, g0 + br * nb + j))
    w_spec = lambda w: pl.BlockSpec((None, w.shape[1], tn), lambda i, j: (layer, 0, j))
    return pl.pallas_call(
        _merge_kernel,
        grid=(m // tm, nb),
        in_specs=[o_spec(oa), o_spec(ob), o_spec(oc), g_spec(0), g_spec(1), g_spec(2),
                  w_spec(wa), w_spec(wb), w_spec(wc)],
        out_specs=pl.BlockSpec((tm, tn), lambda i, j: (i, j)),
        out_shape=jax.ShapeDtypeStruct((m, d), BF16),
        compiler_params=_cparams("parallel", "parallel"),
        name="gated_merge",
    )(oa, ob, oc, hg, hg, hg, wa, wb, wc)


def _resid_proj_kernel(x_ref, y_ref, w_ref, o_ref):
    for c in range(0, o_ref.shape[1], PROJ_CHUNK):
        cs = slice(c, c + PROJ_CHUNK)
        o_ref[:, cs] = x_ref[:, cs] + _dot(y_ref[...], w_ref[:, cs])


def _resid_proj(x2, y, w, layer, in_place):
    m, d = x2.shape
    tm = min(TM_PROJ, m)
    tn = TN_WIDE if d % TN_WIDE == 0 else TN_PROJ
    return pl.pallas_call(
        _resid_proj_kernel,
        grid=(m // tm, d // tn),
        in_specs=[
            pl.BlockSpec((tm, tn), lambda i, j: (i, j)),
            pl.BlockSpec((tm, y.shape[1]), lambda i, j: (i, 0)),
            pl.BlockSpec((None, y.shape[1], tn), lambda i, j: (layer, 0, j)),
        ],
        out_specs=pl.BlockSpec((tm, tn), lambda i, j: (i, j)),
        out_shape=jax.ShapeDtypeStruct((m, d), F32),
        input_output_aliases={0: 0} if in_place else {},
        compiler_params=_cparams("parallel", "parallel"),
        name="resid_proj",
    )(x2, y, w)


def _ffn_kernel(x_ref, g_ref, wg_ref, wu_ref, wd_ref, o_ref, xn_ref):
    @pl.when(pl.program_id(1) == 0)
    def _():
        x = x_ref[...]
        xn_ref[...] = _rms_rows(x, g_ref[...]).astype(BF16)
        o_ref[...] = x

    xn = xn_ref[...]
    acc = None
    for c in range(0, wg_ref.shape[1], PROJ_CHUNK):
        a = _dot(xn, wg_ref[:, c:c + PROJ_CHUNK])
        u = _dot(xn, wu_ref[:, c:c + PROJ_CHUNK])
        part = _dot((_silu(a) * u).astype(BF16), wd_ref[c:c + PROJ_CHUNK, :])
        acc = part if acc is None else acc + part
    o_ref[...] += acc


def _ffn(x2, g, wg, wu, wd, layer):
    m, d = x2.shape
    f = wg.shape[2]
    tm = min(TM_FFN, m)
    tf = min(TF_FFN, f)
    return pl.pallas_call(
        _ffn_kernel,
        grid=(m // tm, f // tf),
        in_specs=[
            pl.BlockSpec((tm, d), lambda i, j: (i, 0)),
            pl.BlockSpec((1, d), lambda i, j: (0, 0)),
            pl.BlockSpec((None, d, tf), lambda i, j: (layer, 0, j)),
            pl.BlockSpec((None, d, tf), lambda i, j: (layer, 0, j)),
            pl.BlockSpec((None, tf, d), lambda i, j: (layer, j, 0)),
        ],
        out_specs=pl.BlockSpec((tm, d), lambda i, j: (i, 0)),
        out_shape=jax.ShapeDtypeStruct((m, d), F32),
        scratch_shapes=[pltpu.VMEM((tm, d), BF16)],
        input_output_aliases={0: 0},
        compiler_params=_cparams("parallel", "arbitrary"),
        name="ffn",
    )(x2, g, wg, wu, wd)


def _row(v):
    return v.astype(F32).reshape(1, -1)


def kernel(x, norm_mix, w_in, b_forget, q_norm_diff, k_norm_diff, lambda_q1, lambda_k1, lambda_q2, lambda_k2, sub_norm_diff, q_norm_fox, k_norm_fox, w_branch_sb, w_branch_diff, w_branch_fox, w_gate, w_out, norm_ffn, w_ff_gate, w_ff_up, w_ff_down):
    b, s, d = x.shape
    depth = w_in.shape[0]
    d_main = HEAD_DIM * 3 * (H_SB + H_DIFF + H_FOX)
    slopes = 2.0 ** (-ALIBI_MAX * jnp.arange(1, H_DIFF + 1, dtype=F32) / H_DIFF)
    slopes = jnp.broadcast_to(slopes[:, None, None], (H_DIFF, 1, LANES))

    w_in_b, w_gate_b, w_out_b = w_in.astype(BF16), w_gate.astype(BF16), w_out.astype(BF16)
    w_f_b = jnp.pad(w_in[:, :, d_main:], ((0, 0), (0, 0), (0, LANES - H_FOX))).astype(BF16)
    w_sb_b, w_diff_b, w_fox_b = w_branch_sb.astype(BF16), w_branch_diff.astype(BF16), w_branch_fox.astype(BF16)
    w_ffg_b, w_ffu_b, w_ffd_b = w_ff_gate.astype(BF16), w_ff_up.astype(BF16), w_ff_down.astype(BF16)
    col_scale = jnp.where(jnp.arange(d_main) < H_SB * HEAD_DIM, HEAD_DIM ** -0.5 * LOG2E, 1.0).astype(F32)[None, :]

    x2 = x.reshape(b * s, d)
    for l in range(depth):
        lambda_init = 0.8 - 0.6 * float(np.exp(-0.3 * l))
        b_pad = jnp.pad(_row(b_forget[l]), ((0, 0), (0, LANES - H_FOX)))

        hg, hf = _in_proj(x2, _row(norm_mix[l]), col_scale, w_in_b, w_gate_b, w_f_b, l)

        cum, cum_col = _forget_cum(hf.reshape(b, s, LANES), b_pad)
        h3 = hg.reshape(b, s, hg.shape[1])

        o_a = _stick_breaking(h3, b, s)
        lam_params = jnp.stack([lambda_q1[l], lambda_k1[l], lambda_q2[l], lambda_k2[l]]).astype(F32)
        o_b = _diff_attention(h3, b, s,
                              _row(jnp.tile(q_norm_diff[l], 2)), _row(jnp.tile(k_norm_diff[l], 2)),
                              _row(sub_norm_diff[l]), lam_params, slopes, lambda_init)
        o_c = _fox_attention(h3, b, s, _row(q_norm_fox[l]), _row(k_norm_fox[l]),
                             cum.reshape(b, SUBLANES, 1, s), cum_col)

        y = _gated_merge(o_a.reshape(b * s, -1), o_b.reshape(b * s, -1), o_c.reshape(b * s, -1), hg, d_main,
                         w_sb_b, w_diff_b, w_fox_b, l)
        x2 = _resid_proj(x2, y, w_out_b, l, in_place=l > 0)
        x2 = _ffn(x2, _row(norm_ffn[l]), w_ffg_b, w_ffu_b, w_ffd_b, l)
    return x2.reshape(b, s, d)
```

```python
import functools

import jax
import jax.numpy as jnp
import numpy as np
from jax import lax
from jax.experimental import pallas as pl
from jax.experimental.pallas import tpu as pltpu

F32 = jnp.float32
BF16 = jnp.bfloat16

HEAD_DIM = 128
DIFF_DIM = HEAD_DIM // 2
H_SB = 6
H_DIFF = 5
H_FOX = 5
CHUNK = 64
EPS = 1e-6
ALIBI_MAX = 8.0

LANES = 128
SUBLANES = 8
VMEM_LIMIT_BYTES = 56 * 1024 * 1024

TM_PROJ = 1024
TN_PROJ = 512
TN_IN_PROJ = 1536
TN_WIDE = 1024
PROJ_CHUNK = 256
TM_FFN = 1024
TF_FFN = 512
TQ = 256
INTERLEAVE_SB = 4
INTERLEAVE_DIFF = 3
INTERLEAVE_FOX = 3
INTERLEAVE_BOUNDED = 2
NEG = -1e30
MAX_LOGIT_BOUND = 48.0
LOG2E = float(np.log2(np.e))


def _cparams(*sem):
    return pltpu.CompilerParams(dimension_semantics=sem, vmem_limit_bytes=VMEM_LIMIT_BYTES)


def _rms_rows(x, g):
    ms = jnp.mean(x * x, axis=-1, keepdims=True)
    return x * lax.rsqrt(ms + EPS) * g


def _sigmoid(z):
    return 0.5 * jnp.tanh(0.5 * z) + 0.5


def _silu(z):
    h = 0.5 * z
    return h * jnp.tanh(h) + h


def _interleave(tasks, width):
    tasks = iter(tasks)
    active = [t for _, t in zip(range(width), tasks)]
    while active:
        for t in list(active):
            try:
                next(t)
            except StopIteration:
                active.remove(t)
                nxt = next(tasks, None)
                if nxt is not None:
                    active.append(nxt)


def _dot_nt(a, b):
    return lax.dot_general(a, b, (((1,), (1,)), ((), ())), preferred_element_type=F32)


def _dot(a, b):
    return jnp.dot(a, b, preferred_element_type=F32)


def _in_proj_kernel(x_ref, g_ref, cs_ref, w_ref, wg_ref, wf_ref, o_ref, hf_ref, xn_ref, *, n_plain):
    j = pl.program_id(1)

    @pl.when(j == 0)
    def _():
        xn_ref[...] = _rms_rows(x_ref[...], g_ref[...]).astype(BF16)
        hf_ref[...] = _dot(xn_ref[...], wf_ref[...])

    def project(wt_ref, act):
        for c in range(0, o_ref.shape[1], PROJ_CHUNK):
            cs = slice(c, c + PROJ_CHUNK)
            o_ref[:, cs] = act(_dot(xn_ref[...], wt_ref[:, cs]), cs).astype(o_ref.dtype)

    @pl.when(j < n_plain)
    def _():
        project(w_ref, lambda a, cs: a * cs_ref[:, cs])

    @pl.when(j >= n_plain)
    def _():
        project(wg_ref, lambda a, cs: _sigmoid(a))


def _in_proj(x2, g, col_scale, w_in, w_gate, w_f, layer):
    m, d = x2.shape
    d_plain, d_gate = col_scale.shape[1], w_gate.shape[2]
    tm = min(TM_PROJ, m)
    tn = TN_IN_PROJ if (d_gate % TN_IN_PROJ == 0 and d_plain % TN_IN_PROJ == 0) else TN_PROJ
    n_plain = d_plain // tn
    return pl.pallas_call(
        functools.partial(_in_proj_kernel, n_plain=n_plain),
        grid=(m // tm, (d_plain + d_gate) // tn),
        in_specs=[
            pl.BlockSpec((tm, d), lambda i, j: (i, 0)),
            pl.BlockSpec((1, d), lambda i, j: (0, 0)),
            pl.BlockSpec((1, tn), lambda i, j: (0, jnp.minimum(j, n_plain - 1))),
            pl.BlockSpec((None, d, tn), lambda i, j: (layer, 0, jnp.minimum(j, n_plain - 1))),
            pl.BlockSpec((None, d, tn), lambda i, j: (layer, 0, jnp.maximum(j - n_plain, 0))),
            pl.BlockSpec((None, d, LANES), lambda i, j: (layer, 0, 0)),
        ],
        out_specs=[pl.BlockSpec((tm, tn), lambda i, j: (i, j)),
                   pl.BlockSpec((tm, LANES), lambda i, j: (i, 0))],
        out_shape=[jax.ShapeDtypeStruct((m, d_plain + d_gate), BF16), jax.ShapeDtypeStruct((m, LANES), F32)],
        scratch_shapes=[pltpu.VMEM((tm, d), BF16)],
        compiler_params=_cparams("parallel", "arbitrary"),
        name="in_proj",
    )(x2, g, col_scale, w_in, w_gate, w_f)


def _split3(x):
    p0 = x.astype(BF16)
    r0 = x - p0.astype(F32)
    p1 = r0.astype(BF16)
    return p0, p1, (r0 - p1.astype(F32)).astype(BF16)


def _forget_cum_kernel(hf_ref, b_ref, o_ref, c_ref, *, blk):
    s = hf_ref.shape[1]
    z = hf_ref[0] + b_ref[...]
    lf = jnp.minimum(z, 0.0) - jnp.log(1.0 + jnp.exp(-jnp.abs(z)))
    lft = lf.T[:SUBLANES]
    row = lax.broadcasted_iota(jnp.int32, (blk, blk), 0)
    col = lax.broadcasted_iota(jnp.int32, (blk, blk), 1)
    upper = (row <= col).astype(BF16)
    lower = (row >= col).astype(BF16)
    carry = jnp.zeros((SUBLANES, 1), F32)
    carry_c = jnp.zeros((1, LANES), F32)
    for i in range(s // blk):
        p0, p1, p2 = _split3(lft[:, i * blk:(i + 1) * blk])
        cs = _dot(p0, upper) + _dot(p1, upper) + _dot(p2, upper) + carry
        o_ref[0, :, i * blk:(i + 1) * blk] = cs
        carry = cs[:, blk - 1:blk]
        p0, p1, p2 = _split3(lf[i * blk:(i + 1) * blk, :])
        cc = _dot(lower, p0) + _dot(lower, p1) + _dot(lower, p2) + carry_c
        c_ref[0, i * blk:(i + 1) * blk, :] = cc
        carry_c = cc[blk - 1:blk, :]


def _forget_cum(hf, b_pad):
    b, s, _ = hf.shape
    blk = min(256, s)
    return pl.pallas_call(
        functools.partial(_forget_cum_kernel, blk=blk),
        grid=(b,),
        in_specs=[
            pl.BlockSpec((1, s, LANES), lambda i: (i, 0, 0)),
            pl.BlockSpec((1, LANES), lambda i: (0, 0)),
        ],
        out_specs=[pl.BlockSpec((1, SUBLANES, s), lambda i: (i, 0, 0)),
                   pl.BlockSpec((1, s, LANES), lambda i: (i, 0, 0))],
        out_shape=[jax.ShapeDtypeStruct((b, SUBLANES, s), F32), jax.ShapeDtypeStruct((b, s, LANES), F32)],
        compiler_params=_cparams("parallel"),
        name="forget_cum",
    )(hf, b_pad)


def _sb_kernel(q_ref, k_ref, v_ref, o_ref, *, tq):
    s = q_ref.shape[1]
    row = lax.broadcasted_iota(jnp.int32, (tq, tq), 0)
    col = lax.broadcasted_iota(jnp.int32, (tq, tq), 1)
    neg_tri = -((row >= col).astype(BF16))
    before = col < row

    def q_block(i):
        q0 = i * tq
        q = q_ref[0, q0:q0 + tq, :]
        zs, sufs = [], []
        for j in range(i + 1):
            z = _dot_nt(q, k_ref[0, j * tq:(j + 1) * tq, :])
            sp = jnp.maximum(z, 0.0) + jnp.log2(1.0 + jnp.exp2(-jnp.abs(z)))
            if j == i:
                sp = jnp.where(before, sp, 0.0)
            zs.append(z)
            yield
            sufs.append(_dot(sp.astype(BF16), neg_tri))
            yield
        carry = None
        acc = None
        for j in reversed(range(i + 1)):
            arg = zs[j] + sufs[j]
            if carry is not None:
                arg = arg + carry
            if j == i:
                arg = jnp.where(before, arg, NEG)
            pv = _dot(jnp.exp2(arg).astype(BF16), v_ref[0, j * tq:(j + 1) * tq, :])
            acc = pv if acc is None else acc + pv
            tot = sufs[j][:, 0:1]
            carry = tot if carry is None else carry + tot
            yield
        o_ref[0, q0:q0 + tq, :] = acc.astype(o_ref.dtype)

    _interleave([q_block(i) for i in range(s // tq)], INTERLEAVE_SB)


def _stick_breaking(h3, b, s):
    tq = min(TQ, s)
    blk = lambda off: pl.BlockSpec((1, s, HEAD_DIM), lambda bi, hi: (bi, 0, off + hi))
    return pl.pallas_call(
        functools.partial(_sb_kernel, tq=tq),
        grid=(b, H_SB),
        in_specs=[blk(0), blk(H_SB), blk(2 * H_SB)],
        out_specs=pl.BlockSpec((1, s, HEAD_DIM), lambda bi, hi: (bi, 0, hi)),
        out_shape=jax.ShapeDtypeStruct((b, s, H_SB * HEAD_DIM), BF16),
        compiler_params=_cparams("parallel", "parallel"),
        name="stick_breaking",
    )(h3, h3, h3)


def _halves_rms(x, g2, lo_lane):
    x2 = x * x
    s_lo = jnp.sum(jnp.where(lo_lane, x2, 0.0), axis=-1, keepdims=True)
    s_hi = jnp.sum(jnp.where(lo_lane, 0.0, x2), axis=-1, keepdims=True)
    inv = jnp.where(lo_lane, lax.rsqrt(s_lo / DIFF_DIM + EPS), lax.rsqrt(s_hi / DIFF_DIM + EPS))
    return x * inv * g2


def _diff_kernel(q_ref, k_ref, v_ref, gq_ref, gk_ref, gs_ref, lam_ref, slope_ref, mb_ref, o_ref,
                 kn_ref, vx_ref, *bias_ref, tq, lambda_init, bounded):
    s = q_ref.shape[1]
    lo_lane = lax.broadcasted_iota(jnp.int32, (1, HEAD_DIM), 1) < DIFF_DIM
    row = lax.broadcasted_iota(jnp.int32, (tq, tq), 0)
    col = lax.broadcasted_iota(jnp.int32, (tq, tq), 1)
    visible = (col // CHUNK) <= (row // CHUNK)
    slope = slope_ref[0][:, 0:1] * LOG2E
    diag_bias = (row - jnp.abs(row - col)).astype(F32) * slope
    key_pos = lax.broadcasted_iota(jnp.int32, (1, s), 1).astype(F32) * slope
    lp = lam_ref[...]
    lam = (jnp.exp(jnp.sum(lp[0:1] * lp[1:2], axis=-1, keepdims=True))
           - jnp.exp(jnp.sum(lp[2:3] * lp[3:4], axis=-1, keepdims=True)) + lambda_init)

    kn_ref[...] = _halves_rms(k_ref[0].astype(F32), gk_ref[...], lo_lane).astype(BF16)
    vx_ref[:, :HEAD_DIM] = v_ref[0]
    vx_ref[:, HEAD_DIM:] = jnp.ones((s, HEAD_DIM), BF16)

    def rowmax(x):
        return jnp.max(x, axis=-1, keepdims=True)

    def rowsum(x):
        return jnp.sum(x, axis=-1, keepdims=True)

    def q_block(i):
        q0 = i * tq
        qn = _halves_rms(q_ref[0, q0:q0 + tq, :].astype(F32), gq_ref[...], lo_lane) * (DIFF_DIM ** -0.5 * LOG2E)
        q12 = jnp.concatenate([jnp.where(lo_lane, qn, 0.0), jnp.where(lo_lane, 0.0, qn)], axis=0).astype(BF16)
        bo = key_pos[:, :q0 + tq] - slope * q0
        s1, s2 = [], []
        for j in range(i + 1):
            z = _dot_nt(q12, kn_ref[j * tq:(j + 1) * tq, :])
            if j == i:
                s1.append(jnp.where(visible, z[:tq] + diag_bias, NEG))
                s2.append(jnp.where(visible, z[tq:] + diag_bias, NEG))
            else:
                s1.append(z[:tq] + bo[:, j * tq:(j + 1) * tq])
                s2.append(z[tq:] + bo[:, j * tq:(j + 1) * tq])
            yield
        m1 = rowmax(functools.reduce(jnp.maximum, s1))
        m2 = rowmax(functools.reduce(jnp.maximum, s2))
        acc = None
        for j in range(i + 1):
            p12 = jnp.concatenate([jnp.exp2(s1[j] - m1), jnp.exp2(s2[j] - m2)], axis=0).astype(BF16)
            pv = _dot(p12, vx_ref[j * tq:(j + 1) * tq, :])
            acc = pv if acc is None else acc + pv
            yield
        finish(q0, acc)

    def finish(q0, acc):
        o = (acc[:tq, :HEAD_DIM] / acc[:tq, HEAD_DIM:]
             - acc[tq:, :HEAD_DIM] * (lam / acc[tq:, HEAD_DIM:]))
        o = _rms_rows(o, gs_ref[...]) * (1.0 - lambda_init)
        o_ref[0, q0:q0 + tq, :] = o.astype(o_ref.dtype)

    def q_block_bounded(i):
        q0 = i * tq
        qn = _halves_rms(q_ref[0, q0:q0 + tq, :].astype(F32), gq_ref[...], lo_lane) * (DIFF_DIM ** -0.5 * LOG2E)
        q12 = jnp.concatenate([jnp.where(lo_lane, qn, 0.0), jnp.where(lo_lane, 0.0, qn)], axis=0).astype(BF16)
        acc = None
        for j in range(i + 1):
            z = _dot_nt(q12, kn_ref[j * tq:(j + 1) * tq, :])
            bias = bias_ref[0][i - j]
            p12 = jnp.exp2(jnp.concatenate([z[:tq] + bias, z[tq:] + bias], axis=0)).astype(BF16)
            pv = _dot(p12, vx_ref[j * tq:(j + 1) * tq, :])
            acc = pv if acc is None else acc + pv
            yield
        finish(q0, acc)

    if bounded:
        dist = (row - col).astype(F32)
        mb = mb_ref[...]
        bias_ref[0][0] = jnp.where(visible, -slope * jnp.abs(dist) - mb, NEG)
        for dlt in range(1, s // tq):
            bias_ref[0][dlt] = -slope * (dist + float(dlt * tq)) - mb
        _interleave([q_block_bounded(i) for i in range(s // tq)], INTERLEAVE_BOUNDED)
    else:
        _interleave([q_block(i) for i in range(s // tq)], INTERLEAVE_DIFF)


def _logit_bound(gq, gk, dim):
    return (jnp.max(jnp.abs(gq)) * jnp.max(jnp.abs(gk)) * (dim ** 0.5 * LOG2E * 1.02)).astype(F32).reshape(1, 1)


def _diff_attention(h3, b, s, gq2, gk2, gs, lam_params, slopes, lambda_init):
    tq = min(TQ, s)
    off = 3 * H_SB
    blk = lambda o: pl.BlockSpec((1, s, HEAD_DIM), lambda bi, hi: (bi, 0, o + hi))
    vec = pl.BlockSpec((1, HEAD_DIM), lambda bi, hi: (0, 0))
    one = pl.BlockSpec((1, 1), lambda bi, hi: (0, 0))
    mb = _logit_bound(gq2, gk2, DIFF_DIM)

    def call(bounded):
        bias = [pltpu.VMEM((s // tq, tq, tq), F32)] if bounded else []
        return pl.pallas_call(
            functools.partial(_diff_kernel, tq=tq, lambda_init=lambda_init, bounded=bounded),
            grid=(b, H_DIFF),
            in_specs=[blk(off), blk(off + H_DIFF), blk(off + 2 * H_DIFF), vec, vec, vec,
                      pl.BlockSpec((4, DIFF_DIM), lambda bi, hi: (0, 0)),
                      pl.BlockSpec((1, 1, LANES), lambda bi, hi: (hi, 0, 0)), one],
            out_specs=pl.BlockSpec((1, s, HEAD_DIM), lambda bi, hi: (bi, 0, hi)),
            out_shape=jax.ShapeDtypeStruct((b, s, H_DIFF * HEAD_DIM), BF16),
            scratch_shapes=[pltpu.VMEM((s, HEAD_DIM), BF16), pltpu.VMEM((s, 2 * HEAD_DIM), BF16)] + bias,
            compiler_params=_cparams("parallel", "parallel"),
            name="diff_attention_bounded" if bounded else "diff_attention",
        )

    return lax.cond(mb[0, 0] < MAX_LOGIT_BOUND, call(True), call(False),
                    h3, h3, h3, gq2, gk2, gs, lam_params, slopes, mb)


def _fox_kernel(q_ref, k_ref, v_ref, gq_ref, gk_ref, cum_ref, cumc_ref, mb_ref, o_ref, kn_ref, *vx_ref,
                tq, bounded):
    s = q_ref.shape[1]
    scale = HEAD_DIM ** -0.5
    row = lax.broadcasted_iota(jnp.int32, (tq, tq), 0)
    col = lax.broadcasted_iota(jnp.int32, (tq, tq), 1)
    causal = col <= row

    kn_ref[...] = _rms_rows(k_ref[0].astype(F32), gk_ref[...]).astype(BF16)

    def q_block(i):
        q0 = i * tq
        q = (_rms_rows(q_ref[0, q0:q0 + tq, :].astype(F32), gq_ref[...]) * (scale * LOG2E)).astype(BF16)
        cum = cum_ref[0, 0, :, 0:q0 + tq]
        bias = (cum[:, q0:q0 + 1] - cum) * LOG2E
        ss = []
        for j in range(i + 1):
            sj = _dot_nt(q, kn_ref[j * tq:(j + 1) * tq, :]) + bias[:, j * tq:(j + 1) * tq]
            ss.append(jnp.where(causal, sj, NEG) if j == i else sj)
            yield
        m = jnp.max(functools.reduce(jnp.maximum, ss), axis=-1, keepdims=True)
        ps = []
        for sj in ss:
            ps.append(jnp.exp2(sj - m))
            yield
        l = jnp.sum(functools.reduce(jnp.add, ps), axis=-1, keepdims=True)
        acc = None
        for j, pj in enumerate(ps):
            pv = _dot(pj.astype(BF16), v_ref[0, j * tq:(j + 1) * tq, :])
            acc = pv if acc is None else acc + pv
            yield
        o_ref[0, q0:q0 + tq, :] = (acc / l).astype(o_ref.dtype)

    def q_block_bounded(i):
        q0 = i * tq
        q = (_rms_rows(q_ref[0, q0:q0 + tq, :].astype(F32), gq_ref[...]) * (scale * LOG2E)).astype(BF16)
        head = lax.broadcasted_iota(jnp.int32, (1, LANES), 1) == pl.program_id(1)
        f_t = jnp.sum(jnp.where(head, cumc_ref[0, q0:q0 + tq, :], 0.0), axis=-1, keepdims=True)
        row_term = f_t * LOG2E - mb_ref[...]
        col_term = cum_ref[0, 0, :, 0:q0 + tq] * LOG2E
        acc = None
        for j in range(i + 1):
            sj = _dot_nt(q, kn_ref[j * tq:(j + 1) * tq, :]) + row_term - col_term[:, j * tq:(j + 1) * tq]
            if j == i:
                sj = jnp.where(causal, sj, NEG)
            pv = _dot(jnp.exp2(sj).astype(BF16), vx_ref[0][j * tq:(j + 1) * tq, :])
            acc = pv if acc is None else acc + pv
            yield
        o_ref[0, q0:q0 + tq, :] = (acc[:, :HEAD_DIM] / acc[:, HEAD_DIM:]).astype(o_ref.dtype)

    if bounded:
        vx_ref[0][:, :HEAD_DIM] = v_ref[0]
        vx_ref[0][:, HEAD_DIM:] = jnp.ones((s, HEAD_DIM), BF16)
        _interleave([q_block_bounded(i) for i in range(s // tq)], INTERLEAVE_BOUNDED)
    else:
        _interleave([q_block(i) for i in range(s // tq)], INTERLEAVE_FOX)


def _fox_attention(h3, b, s, gq, gk, cum4, cum_col):
    tq = min(TQ, s)
    off = 3 * H_SB + 3 * H_DIFF
    blk = lambda o: pl.BlockSpec((1, s, HEAD_DIM), lambda bi, hi: (bi, 0, o + hi))
    vec = pl.BlockSpec((1, HEAD_DIM), lambda bi, hi: (0, 0))
    mb = _logit_bound(gq, gk, HEAD_DIM)

    def call(bounded):
        vx = [pltpu.VMEM((s, 2 * HEAD_DIM), BF16)] if bounded else []
        return pl.pallas_call(
            functools.partial(_fox_kernel, tq=tq, bounded=bounded),
            grid=(b, H_FOX),
            in_specs=[blk(off), blk(off + H_FOX), blk(off + 2 * H_FOX), vec, vec,
                      pl.BlockSpec((1, 1, 1, s), lambda bi, hi: (bi, hi, 0, 0)),
                      pl.BlockSpec((1, s, LANES), lambda bi, hi: (bi, 0, 0)),
                      pl.BlockSpec((1, 1), lambda bi, hi: (0, 0))],
            out_specs=pl.BlockSpec((1, s, HEAD_DIM), lambda bi, hi: (bi, 0, hi)),
            out_shape=jax.ShapeDtypeStruct((b, s, H_FOX * HEAD_DIM), BF16),
            scratch_shapes=[pltpu.VMEM((s, HEAD_DIM), BF16)] + vx,
            compiler_params=_cparams("parallel", "parallel"),
            name="fox_attention_bounded" if bounded else "fox_attention",
        )

    return lax.cond(mb[0, 0] < MAX_LOGIT_BOUND, call(True), call(False), h3, h3, h3, gq, gk, cum4, cum_col, mb)


def _merge_kernel(oa_ref, ob_ref, oc_ref, ga_ref, gb_ref, gc_ref, wa_ref, wb_ref, wc_ref, y_ref):
    for c in range(0, y_ref.shape[1], PROJ_CHUNK):
        cs = slice(c, c + PROJ_CHUNK)
        y = ga_ref[:, cs].astype(F32) * _dot(oa_ref[...], wa_ref[:, cs])
        y += gb_ref[:, cs].astype(F32) * _dot(ob_ref[...], wb_ref[:, cs])
        y += gc_ref[:, cs].astype(F32) * _dot(oc_ref[...], wc_ref[:, cs])
        y_ref[:, cs] = y.astype(y_ref.dtype)


def _gated_merge(oa, ob, oc, hg, gate_col, wa, wb, wc, layer):
    m = oa.shape[0]
    d = wa.shape[2]
    tm = min(TM_PROJ, m)
    tn = TN_WIDE if d % TN_WIDE == 0 else TN_PROJ
    nb = d // tn
    g0 = gate_col // tn
    o_spec = lambda a: pl.BlockSpec((tm, a.shape[1]), lambda i, j: (i, 0))
    g_spec = lambda br: pl.BlockSpec((tm, tn), lambda i, j: (i, g0 + br * nb + j))
    w_spec = lambda w: pl.BlockSpec((None, w.shape[1], tn), lambda i, j: (layer, 0, j))
    return pl.pallas_call(
        _merge_kernel,
        grid=(m // tm, nb),
        in_specs=[o_spec(oa), o_spec(ob), o_spec(oc), g_spec(0), g_spec(1), g_spec(2),
                  w_spec(wa), w_spec(wb), w_spec(wc)],
        out_specs=pl.BlockSpec((tm, tn), lambda i, j: (i, j)),
        out_shape=jax.ShapeDtypeStruct((m, d), BF16),
        compiler_params=_cparams("parallel", "parallel"),
        name="gated_merge",
    )(oa, ob, oc, hg, hg, hg, wa, wb, wc)


def _resid_proj_kernel(x_ref, y_ref, w_ref, o_ref):
    for c in range(0, o_ref.shape[1], PROJ_CHUNK):
        cs = slice(c, c + PROJ_CHUNK)
        o_ref[:, cs] = x_ref[:, cs] + _dot(y_ref[...], w_ref[:, cs])


def _resid_proj(x2, y, w, layer, in_place):
    m, d = x2.shape
    tm = min(TM_PROJ, m)
    tn = TN_WIDE if d % TN_WIDE == 0 else TN_PROJ
    return pl.pallas_call(
        _resid_proj_kernel,
        grid=(m // tm, d // tn),
        in_specs=[
            pl.BlockSpec((tm, tn), lambda i, j: (i, j)),
            pl.BlockSpec((tm, y.shape[1]), lambda i, j: (i, 0)),
            pl.BlockSpec((None, y.shape[1], tn), lambda i, j: (layer, 0, j)),
        ],
        out_specs=pl.BlockSpec((tm, tn), lambda i, j: (i, j)),
        out_shape=jax.ShapeDtypeStruct((m, d), F32),
        input_output_aliases={0: 0} if in_place else {},
        compiler_params=_cparams("parallel", "parallel"),
        name="resid_proj",
    )(x2, y, w)


def _ffn_kernel(x_ref, g_ref, wg_ref, wu_ref, wd_ref, o_ref, xn_ref):
    @pl.when(pl.program_id(1) == 0)
    def _():
        x = x_ref[...]
        xn_ref[...] = _rms_rows(x, g_ref[...]).astype(BF16)
        o_ref[...] = x

    xn = xn_ref[...]
    acc = None
    for c in range(0, wg_ref.shape[1], PROJ_CHUNK):
        a = _dot(xn, wg_ref[:, c:c + PROJ_CHUNK])
        u = _dot(xn, wu_ref[:, c:c + PROJ_CHUNK])
        part = _dot((_silu(a) * u).astype(BF16), wd_ref[c:c + PROJ_CHUNK, :])
        acc = part if acc is None else acc + part
    o_ref[...] += acc


def _ffn(x2, g, wg, wu, wd, layer):
    m, d = x2.shape
    f = wg.shape[2]
    tm = min(TM_FFN, m)
    tf = min(TF_FFN, f)
    return pl.pallas_call(
        _ffn_kernel,
        grid=(m // tm, f // tf),
        in_specs=[
            pl.BlockSpec((tm, d), lambda i, j: (i, 0)),
            pl.BlockSpec((1, d), lambda i, j: (0, 0)),
            pl.BlockSpec((None, d, tf), lambda i, j: (layer, 0, j)),
            pl.BlockSpec((None, d, tf), lambda i, j: (layer, 0, j)),
            pl.BlockSpec((None, tf, d), lambda i, j: (layer, j, 0)),
        ],
        out_specs=pl.BlockSpec((tm, d), lambda i, j: (i, 0)),
        out_shape=jax.ShapeDtypeStruct((m, d), F32),
        scratch_shapes=[pltpu.VMEM((tm, d), BF16)],
        input_output_aliases={0: 0},
        compiler_params=_cparams("parallel", "arbitrary"),
        name="ffn",
    )(x2, g, wg, wu, wd)


def _row(v):
    return v.astype(F32).reshape(1, -1)


def kernel(x, norm_mix, w_in, b_forget, q_norm_diff, k_norm_diff, lambda_q1, lambda_k1, lambda_q2, lambda_k2, sub_norm_diff, q_norm_fox, k_norm_fox, w_branch_sb, w_branch_diff, w_branch_fox, w_gate, w_out, norm_ffn, w_ff_gate, w_ff_up, w_ff_down):
    b, s, d = x.shape
    depth = w_in.shape[0]
    d_main = HEAD_DIM * 3 * (H_SB + H_DIFF + H_FOX)
    slopes = 2.0 ** (-ALIBI_MAX * jnp.arange(1, H_DIFF + 1, dtype=F32) / H_DIFF)
    slopes = jnp.broadcast_to(slopes[:, None, None], (H_DIFF, 1, LANES))

    w_in_b, w_gate_b, w_out_b = w_in[:, :, :d_main].astype(BF16), w_gate.astype(BF16), w_out.astype(BF16)
    w_f_b = jnp.pad(w_in[:, :, d_main:], ((0, 0), (0, 0), (0, LANES - H_FOX))).astype(BF16)
    w_sb_b, w_diff_b, w_fox_b = w_branch_sb.astype(BF16), w_branch_diff.astype(BF16), w_branch_fox.astype(BF16)
    w_ffg_b, w_ffu_b, w_ffd_b = w_ff_gate.astype(BF16), w_ff_up.astype(BF16), w_ff_down.astype(BF16)
    col_scale = jnp.where(jnp.arange(d_main) < H_SB * HEAD_DIM, HEAD_DIM ** -0.5 * LOG2E, 1.0).astype(F32)[None, :]

    x2 = x.reshape(b * s, d)
    for l in range(depth):
        lambda_init = 0.8 - 0.6 * float(np.exp(-0.3 * l))
        b_pad = jnp.pad(_row(b_forget[l]), ((0, 0), (0, LANES - H_FOX)))

        hg, hf = _in_proj(x2, _row(norm_mix[l]), col_scale, w_in_b, w_gate_b, w_f_b, l)

        cum, cum_col = _forget_cum(hf.reshape(b, s, LANES), b_pad)
        h3 = hg.reshape(b, s, hg.shape[1])

        o_a = _stick_breaking(h3, b, s)
        lam_params = jnp.stack([lambda_q1[l], lambda_k1[l], lambda_q2[l], lambda_k2[l]]).astype(F32)
        o_b = _diff_attention(h3, b, s,
                              _row(jnp.tile(q_norm_diff[l], 2)), _row(jnp.tile(k_norm_diff[l], 2)),
                              _row(sub_norm_diff[l]), lam_params, slopes, lambda_init)
        o_c = _fox_attention(h3, b, s, _row(q_norm_fox[l]), _row(k_norm_fox[l]),
                             cum.reshape(b, SUBLANES, 1, s), cum_col)

        y = _gated_merge(o_a.reshape(b * s, -1), o_b.reshape(b * s, -1), o_c.reshape(b * s, -1), hg, d_main,
                         w_sb_b, w_diff_b, w_fox_b, l)
        x2 = _resid_proj(x2, y, w_out_b, l, in_place=l > 0)
        x2 = _ffn(x2, _row(norm_ffn[l]), w_ffg_b, w_ffu_b, w_ffd_b, l)
    return x2.reshape(b, s, d)
```

```python
import functools

import jax
import jax.numpy as jnp
import numpy as np
from jax import lax
from jax.experimental import pallas as pl
from jax.experimental.pallas import tpu as pltpu

F32 = jnp.float32
BF16 = jnp.bfloat16

HEAD_DIM = 128
DIFF_DIM = HEAD_DIM // 2
H_SB = 6
H_DIFF = 5
H_FOX = 5
CHUNK = 64
EPS = 1e-6
ALIBI_MAX = 8.0

LANES = 128
SUBLANES = 8
VMEM_LIMIT_BYTES = 56 * 1024 * 1024

TM_PROJ = 1024
TN_PROJ = 512
TN_IN_PROJ = 1536
TN_WIDE = 1024
PROJ_CHUNK = 256
TM_FFN = 1024
TF_FFN = 512
TQ = 256
INTERLEAVE_SB = 4
INTERLEAVE_DIFF = 3
INTERLEAVE_FOX = 3
INTERLEAVE_BOUNDED = 2
NEG = -1e30
MAX_LOGIT_BOUND = 48.0
LOG2E = float(np.log2(np.e))


def _cparams(*sem):
    return pltpu.CompilerParams(dimension_semantics=sem, vmem_limit_bytes=VMEM_LIMIT_BYTES)


def _rms_rows(x, g):
    ms = jnp.mean(x * x, axis=-1, keepdims=True)
    return x * lax.rsqrt(ms + EPS) * g


def _sigmoid(z):
    return 0.5 * jnp.tanh(0.5 * z) + 0.5


def _silu(z):
    h = 0.5 * z
    return h * jnp.tanh(h) + h


def _interleave(tasks, width):
    tasks = iter(tasks)
    active = [t for _, t in zip(range(width), tasks)]
    while active:
        for t in list(active):
            try:
                next(t)
            except StopIteration:
                active.remove(t)
                nxt = next(tasks, None)
                if nxt is not None:
                    active.append(nxt)


def _dot_nt(a, b):
    return lax.dot_general(a, b, (((1,), (1,)), ((), ())), preferred_element_type=F32)


def _dot(a, b):
    return jnp.dot(a, b, preferred_element_type=F32)


def _head_norm(hidx):
    diff0 = 3 * H_SB
    fox0 = diff0 + 3 * H_DIFF
    if diff0 <= hidx < diff0 + 2 * H_DIFF:
        return "half"
    if fox0 <= hidx < fox0 + 2 * H_FOX:
        return "full"
    return None


def _in_proj_kernel(x_ref, g_ref, cs_ref, w_ref, wg_ref, wf_ref, o_ref, hf_ref, xn_ref, *, n_plain):
    j = pl.program_id(1)
    tn = o_ref.shape[1]
    lo_lane = lax.broadcasted_iota(jnp.int32, (1, HEAD_DIM), 1) < DIFF_DIM

    @pl.when(j == 0)
    def _():
        xn_ref[...] = _rms_rows(x_ref[...], g_ref[...]).astype(BF16)
        hf_ref[...] = _dot(xn_ref[...], wf_ref[...])

    def project(wt_ref, act, order=None):
        for c in order or range(0, tn, PROJ_CHUNK):
            cs = slice(c, c + PROJ_CHUNK)
            o_ref[:, cs] = act(_dot(xn_ref[...], wt_ref[:, cs]), c).astype(o_ref.dtype)

    def heads(tile):
        def act(a, c):
            outs = []
            for hh in range(0, PROJ_CHUNK, HEAD_DIM):
                sub, gain = a[:, hh:hh + HEAD_DIM], cs_ref[:, c + hh:c + hh + HEAD_DIM]
                norm = _head_norm((tile * tn + c + hh) // HEAD_DIM)
                if norm == "half":
                    outs.append(_halves_rms(sub, gain, lo_lane))
                elif norm == "full":
                    outs.append(_rms_rows(sub, gain))
                else:
                    outs.append(sub * gain)
            return jnp.concatenate(outs, axis=1)
        return act

    def normed(tile, c):
        return any(_head_norm((tile * tn + c + hh) // HEAD_DIM) for hh in range(0, PROJ_CHUNK, HEAD_DIM))

    for tile in range(n_plain):
        @pl.when(j == tile)
        def _(tile=tile):
            order = sorted(range(0, tn, PROJ_CHUNK), key=lambda c: not normed(tile, c))
            project(w_ref, heads(tile), order)

    @pl.when(j >= n_plain)
    def _():
        project(wg_ref, lambda a, c: _sigmoid(a))


def _in_proj(x2, g, col_scale, w_in, w_gate, w_f, layer):
    m, d = x2.shape
    d_plain, d_gate = col_scale.shape[1], w_gate.shape[2]
    tm = min(TM_PROJ, m)
    tn = TN_IN_PROJ if (d_gate % TN_IN_PROJ == 0 and d_plain % TN_IN_PROJ == 0) else TN_PROJ
    n_plain = d_plain // tn
    return pl.pallas_call(
        functools.partial(_in_proj_kernel, n_plain=n_plain),
        grid=(m // tm, (d_plain + d_gate) // tn),
        in_specs=[
            pl.BlockSpec((tm, d), lambda i, j: (i, 0)),
            pl.BlockSpec((1, d), lambda i, j: (0, 0)),
            pl.BlockSpec((1, tn), lambda i, j: (0, jnp.minimum(j, n_plain - 1))),
            pl.BlockSpec((None, d, tn), lambda i, j: (layer, 0, jnp.minimum(j, n_plain - 1))),
            pl.BlockSpec((None, d, tn), lambda i, j: (layer, 0, jnp.maximum(j - n_plain, 0))),
            pl.BlockSpec((None, d, LANES), lambda i, j: (layer, 0, 0)),
        ],
        out_specs=[pl.BlockSpec((tm, tn), lambda i, j: (i, j)),
                   pl.BlockSpec((tm, LANES), lambda i, j: (i, 0))],
        out_shape=[jax.ShapeDtypeStruct((m, d_plain + d_gate), BF16), jax.ShapeDtypeStruct((m, LANES), F32)],
        scratch_shapes=[pltpu.VMEM((tm, d), BF16)],
        compiler_params=_cparams("parallel", "arbitrary"),
        name="in_proj",
    )(x2, g, col_scale, w_in, w_gate, w_f)


def _split3(x):
    p0 = x.astype(BF16)
    r0 = x - p0.astype(F32)
    p1 = r0.astype(BF16)
    return p0, p1, (r0 - p1.astype(F32)).astype(BF16)


def _forget_cum_kernel(hf_ref, b_ref, o_ref, c_ref, *, blk):
    s = hf_ref.shape[1]
    z = hf_ref[0] + b_ref[...]
    lf = jnp.minimum(z, 0.0) - jnp.log(1.0 + jnp.exp(-jnp.abs(z)))
    lft = lf.T[:SUBLANES]
    row = lax.broadcasted_iota(jnp.int32, (blk, blk), 0)
    col = lax.broadcasted_iota(jnp.int32, (blk, blk), 1)
    upper = (row <= col).astype(BF16)
    lower = (row >= col).astype(BF16)
    carry = jnp.zeros((SUBLANES, 1), F32)
    carry_c = jnp.zeros((1, LANES), F32)
    for i in range(s // blk):
        p0, p1, p2 = _split3(lft[:, i * blk:(i + 1) * blk])
        cs = _dot(p0, upper) + _dot(p1, upper) + _dot(p2, upper) + carry
        o_ref[0, :, i * blk:(i + 1) * blk] = cs
        carry = cs[:, blk - 1:blk]
        p0, p1, p2 = _split3(lf[i * blk:(i + 1) * blk, :])
        cc = _dot(lower, p0) + _dot(lower, p1) + _dot(lower, p2) + carry_c
        c_ref[0, i * blk:(i + 1) * blk, :] = cc
        carry_c = cc[blk - 1:blk, :]


def _forget_cum(hf, b_pad):
    b, s, _ = hf.shape
    blk = min(256, s)
    return pl.pallas_call(
        functools.partial(_forget_cum_kernel, blk=blk),
        grid=(b,),
        in_specs=[
            pl.BlockSpec((1, s, LANES), lambda i: (i, 0, 0)),
            pl.BlockSpec((1, LANES), lambda i: (0, 0)),
        ],
        out_specs=[pl.BlockSpec((1, SUBLANES, s), lambda i: (i, 0, 0)),
                   pl.BlockSpec((1, s, LANES), lambda i: (i, 0, 0))],
        out_shape=[jax.ShapeDtypeStruct((b, SUBLANES, s), F32), jax.ShapeDtypeStruct((b, s, LANES), F32)],
        compiler_params=_cparams("parallel"),
        name="forget_cum",
    )(hf, b_pad)


def _sb_kernel(q_ref, k_ref, v_ref, o_ref, *, tq):
    s = q_ref.shape[1]
    row = lax.broadcasted_iota(jnp.int32, (tq, tq), 0)
    col = lax.broadcasted_iota(jnp.int32, (tq, tq), 1)
    neg_tri = -((row >= col).astype(BF16))
    before = col < row

    def q_block(i):
        q0 = i * tq
        q = q_ref[0, q0:q0 + tq, :]
        zs, sufs = [], []
        for j in range(i + 1):
            z = _dot_nt(q, k_ref[0, j * tq:(j + 1) * tq, :])
            sp = jnp.maximum(z, 0.0) + jnp.log2(1.0 + jnp.exp2(-jnp.abs(z)))
            if j == i:
                sp = jnp.where(before, sp, 0.0)
            zs.append(z)
            yield
            sufs.append(_dot(sp.astype(BF16), neg_tri))
            yield
        carry = None
        acc = None
        for j in reversed(range(i + 1)):
            arg = zs[j] + sufs[j]
            if carry is not None:
                arg = arg + carry
            if j == i:
                arg = jnp.where(before, arg, NEG)
            pv = _dot(jnp.exp2(arg).astype(BF16), v_ref[0, j * tq:(j + 1) * tq, :])
            acc = pv if acc is None else acc + pv
            tot = sufs[j][:, 0:1]
            carry = tot if carry is None else carry + tot
            yield
        o_ref[0, q0:q0 + tq, :] = acc.astype(o_ref.dtype)

    _interleave([q_block(i) for i in range(s // tq)], INTERLEAVE_SB)


def _stick_breaking(h3, b, s):
    tq = min(TQ, s)
    blk = lambda off: pl.BlockSpec((1, s, HEAD_DIM), lambda bi, hi: (bi, 0, off + hi))
    return pl.pallas_call(
        functools.partial(_sb_kernel, tq=tq),
        grid=(b, H_SB),
        in_specs=[blk(0), blk(H_SB), blk(2 * H_SB)],
        out_specs=pl.BlockSpec((1, s, HEAD_DIM), lambda bi, hi: (bi, 0, hi)),
        out_shape=jax.ShapeDtypeStruct((b, s, H_SB * HEAD_DIM), BF16),
        compiler_params=_cparams("parallel", "parallel"),
        name="stick_breaking",
    )(h3, h3, h3)


def _halves_rms(x, g2, lo_lane):
    x2 = x * x
    s_lo = jnp.sum(jnp.where(lo_lane, x2, 0.0), axis=-1, keepdims=True)
    s_hi = jnp.sum(jnp.where(lo_lane, 0.0, x2), axis=-1, keepdims=True)
    inv = jnp.where(lo_lane, lax.rsqrt(s_lo / DIFF_DIM + EPS), lax.rsqrt(s_hi / DIFF_DIM + EPS))
    return x * inv * g2


def _diff_kernel(q_ref, k_ref, v_ref, gs_ref, lam_ref, slope_ref, mb_ref, o_ref,
                 vx_ref, *bias_ref, tq, lambda_init, bounded):
    s = q_ref.shape[1]
    lo_lane = lax.broadcasted_iota(jnp.int32, (1, HEAD_DIM), 1) < DIFF_DIM
    row = lax.broadcasted_iota(jnp.int32, (tq, tq), 0)
    col = lax.broadcasted_iota(jnp.int32, (tq, tq), 1)
    visible = (col // CHUNK) <= (row // CHUNK)
    slope = slope_ref[0][:, 0:1] * LOG2E
    diag_bias = (row - jnp.abs(row - col)).astype(F32) * slope
    key_pos = lax.broadcasted_iota(jnp.int32, (1, s), 1).astype(F32) * slope
    lp = lam_ref[...]
    lam = (jnp.exp(jnp.sum(lp[0:1] * lp[1:2], axis=-1, keepdims=True))
           - jnp.exp(jnp.sum(lp[2:3] * lp[3:4], axis=-1, keepdims=True)) + lambda_init)

    vx_ref[:, :HEAD_DIM] = v_ref[0]
    vx_ref[:, HEAD_DIM:] = jnp.ones((s, HEAD_DIM), BF16)

    def rowmax(x):
        return jnp.max(x, axis=-1, keepdims=True)

    def stacked_q(q0):
        q = q_ref[0, q0:q0 + tq, :]
        zero = jnp.zeros_like(q)
        return jnp.concatenate([jnp.where(lo_lane, q, zero), jnp.where(lo_lane, zero, q)], axis=0)

    def q_block(i):
        q0 = i * tq
        q12 = stacked_q(q0)
        bo = key_pos[:, :q0 + tq] - slope * q0
        s1, s2 = [], []
        for j in range(i + 1):
            z = _dot_nt(q12, k_ref[0, j * tq:(j + 1) * tq, :])
            if j == i:
                s1.append(jnp.where(visible, z[:tq] + diag_bias, NEG))
                s2.append(jnp.where(visible, z[tq:] + diag_bias, NEG))
            else:
                s1.append(z[:tq] + bo[:, j * tq:(j + 1) * tq])
                s2.append(z[tq:] + bo[:, j * tq:(j + 1) * tq])
            yield
        m1 = rowmax(functools.reduce(jnp.maximum, s1))
        m2 = rowmax(functools.reduce(jnp.maximum, s2))
        acc = None
        for j in range(i + 1):
            p12 = jnp.concatenate([jnp.exp2(s1[j] - m1), jnp.exp2(s2[j] - m2)], axis=0).astype(BF16)
            pv = _dot(p12, vx_ref[j * tq:(j + 1) * tq, :])
            acc = pv if acc is None else acc + pv
            yield
        finish(q0, acc)

    def finish(q0, acc):
        o = (acc[:tq, :HEAD_DIM] / acc[:tq, HEAD_DIM:]
             - acc[tq:, :HEAD_DIM] * (lam / acc[tq:, HEAD_DIM:]))
        o = _rms_rows(o, gs_ref[...]) * (1.0 - lambda_init)
        o_ref[0, q0:q0 + tq, :] = o.astype(o_ref.dtype)

    def q_block_bounded(i):
        q0 = i * tq
        q12 = stacked_q(q0)
        acc = None
        for j in range(i + 1):
            z = _dot_nt(q12, k_ref[0, j * tq:(j + 1) * tq, :])
            bias = bias_ref[0][i - j]
            p12 = jnp.exp2(jnp.concatenate([z[:tq] + bias, z[tq:] + bias], axis=0)).astype(BF16)
            pv = _dot(p12, vx_ref[j * tq:(j + 1) * tq, :])
            acc = pv if acc is None else acc + pv
            yield
        finish(q0, acc)

    if bounded:
        dist = (row - col).astype(F32)
        mb = mb_ref[...]
        bias_ref[0][0] = jnp.where(visible, -slope * jnp.abs(dist) - mb, NEG)
        for dlt in range(1, s // tq):
            bias_ref[0][dlt] = -slope * (dist + float(dlt * tq)) - mb
        _interleave([q_block_bounded(i) for i in range(s // tq)], INTERLEAVE_BOUNDED)
    else:
        _interleave([q_block(i) for i in range(s // tq)], INTERLEAVE_DIFF)


def _logit_bound(gq, gk, dim):
    return (jnp.max(jnp.abs(gq)) * jnp.max(jnp.abs(gk)) * (dim ** 0.5 * LOG2E * 1.02)).astype(F32).reshape(1, 1)


def _diff_attention(h3, b, s, gq, gk, gs, lam_params, slopes, lambda_init):
    tq = min(TQ, s)
    off = 3 * H_SB
    blk = lambda o: pl.BlockSpec((1, s, HEAD_DIM), lambda bi, hi: (bi, 0, o + hi))
    vec = pl.BlockSpec((1, HEAD_DIM), lambda bi, hi: (0, 0))
    one = pl.BlockSpec((1, 1), lambda bi, hi: (0, 0))
    mb = _logit_bound(gq, gk, DIFF_DIM)

    def call(bounded):
        bias = [pltpu.VMEM((s // tq, tq, tq), F32)] if bounded else []
        return pl.pallas_call(
            functools.partial(_diff_kernel, tq=tq, lambda_init=lambda_init, bounded=bounded),
            grid=(b, H_DIFF),
            in_specs=[blk(off), blk(off + H_DIFF), blk(off + 2 * H_DIFF), vec,
                      pl.BlockSpec((4, DIFF_DIM), lambda bi, hi: (0, 0)),
                      pl.BlockSpec((1, 1, LANES), lambda bi, hi: (hi, 0, 0)), one],
            out_specs=pl.BlockSpec((1, s, HEAD_DIM), lambda bi, hi: (bi, 0, hi)),
            out_shape=jax.ShapeDtypeStruct((b, s, H_DIFF * HEAD_DIM), BF16),
            scratch_shapes=[pltpu.VMEM((s, 2 * HEAD_DIM), BF16)] + bias,
            compiler_params=_cparams("parallel", "parallel"),
            name="diff_attention_bounded" if bounded else "diff_attention",
        )

    return lax.cond(mb[0, 0] < MAX_LOGIT_BOUND, call(True), call(False),
                    h3, h3, h3, gs, lam_params, slopes, mb)


def _fox_kernel(q_ref, k_ref, v_ref, cum_ref, cumc_ref, mb_ref, o_ref, *vx_ref, tq, bounded):
    s = q_ref.shape[1]
    row = lax.broadcasted_iota(jnp.int32, (tq, tq), 0)
    col = lax.broadcasted_iota(jnp.int32, (tq, tq), 1)
    causal = col <= row

    def q_block(i):
        q0 = i * tq
        q = q_ref[0, q0:q0 + tq, :]
        cum = cum_ref[0, 0, :, 0:q0 + tq]
        bias = (cum[:, q0:q0 + 1] - cum) * LOG2E
        ss = []
        for j in range(i + 1):
            sj = _dot_nt(q, k_ref[0, j * tq:(j + 1) * tq, :]) + bias[:, j * tq:(j + 1) * tq]
            ss.append(jnp.where(causal, sj, NEG) if j == i else sj)
            yield
        m = jnp.max(functools.reduce(jnp.maximum, ss), axis=-1, keepdims=True)
        ps = []
        for sj in ss:
            ps.append(jnp.exp2(sj - m))
            yield
        l = jnp.sum(functools.reduce(jnp.add, ps), axis=-1, keepdims=True)
        acc = None
        for j, pj in enumerate(ps):
            pv = _dot(pj.astype(BF16), v_ref[0, j * tq:(j + 1) * tq, :])
            acc = pv if acc is None else acc + pv
            yield
        o_ref[0, q0:q0 + tq, :] = (acc / l).astype(o_ref.dtype)

    def q_block_bounded(i):
        q0 = i * tq
        q = q_ref[0, q0:q0 + tq, :]
        head = lax.broadcasted_iota(jnp.int32, (1, LANES), 1) == pl.program_id(1)
        f_t = jnp.sum(jnp.where(head, cumc_ref[0, q0:q0 + tq, :], 0.0), axis=-1, keepdims=True)
        row_term = f_t * LOG2E - mb_ref[...]
        col_term = cum_ref[0, 0, :, 0:q0 + tq] * LOG2E
        acc = None
        for j in range(i + 1):
            sj = _dot_nt(q, k_ref[0, j * tq:(j + 1) * tq, :]) + row_term - col_term[:, j * tq:(j + 1) * tq]
            if j == i:
                sj = jnp.where(causal, sj, NEG)
            pv = _dot(jnp.exp2(sj).astype(BF16), vx_ref[0][j * tq:(j + 1) * tq, :])
            acc = pv if acc is None else acc + pv
            yield
        o_ref[0, q0:q0 + tq, :] = (acc[:, :HEAD_DIM] / acc[:, HEAD_DIM:]).astype(o_ref.dtype)

    if bounded:
        vx_ref[0][:, :HEAD_DIM] = v_ref[0]
        vx_ref[0][:, HEAD_DIM:] = jnp.ones((s, HEAD_DIM), BF16)
        _interleave([q_block_bounded(i) for i in range(s // tq)], INTERLEAVE_BOUNDED)
    else:
        _interleave([q_block(i) for i in range(s // tq)], INTERLEAVE_FOX)


def _fox_attention(h3, b, s, gq, gk, cum4, cum_col):
    tq = min(TQ, s)
    off = 3 * H_SB + 3 * H_DIFF
    blk = lambda o: pl.BlockSpec((1, s, HEAD_DIM), lambda bi, hi: (bi, 0, o + hi))
    mb = _logit_bound(gq, gk, HEAD_DIM)

    def call(bounded):
        vx = [pltpu.VMEM((s, 2 * HEAD_DIM), BF16)] if bounded else []
        return pl.pallas_call(
            functools.partial(_fox_kernel, tq=tq, bounded=bounded),
            grid=(b, H_FOX),
            in_specs=[blk(off), blk(off + H_FOX), blk(off + 2 * H_FOX),
                      pl.BlockSpec((1, 1, 1, s), lambda bi, hi: (bi, hi, 0, 0)),
                      pl.BlockSpec((1, s, LANES), lambda bi, hi: (bi, 0, 0)),
                      pl.BlockSpec((1, 1), lambda bi, hi: (0, 0))],
            out_specs=pl.BlockSpec((1, s, HEAD_DIM), lambda bi, hi: (bi, 0, hi)),
            out_shape=jax.ShapeDtypeStruct((b, s, H_FOX * HEAD_DIM), BF16),
            scratch_shapes=vx,
            compiler_params=_cparams("parallel", "parallel"),
            name="fox_attention_bounded" if bounded else "fox_attention",
        )

    return lax.cond(mb[0, 0] < MAX_LOGIT_BOUND, call(True), call(False), h3, h3, h3, cum4, cum_col, mb)


def _merge_kernel(oa_ref, ob_ref, oc_ref, ga_ref, gb_ref, gc_ref, wa_ref, wb_ref, wc_ref, y_ref):
    for c in range(0, y_ref.shape[1], PROJ_CHUNK):
        cs = slice(c, c + PROJ_CHUNK)
        y = ga_ref[:, cs].astype(F32) * _dot(oa_ref[...], wa_ref[:, cs])
        y += gb_ref[:, cs].astype(F32) * _dot(ob_ref[...], wb_ref[:, cs])
        y += gc_ref[:, cs].astype(F32) * _dot(oc_ref[...], wc_ref[:, cs])
        y_ref[:, cs] = y.astype(y_ref.dtype)


def _gated_merge(oa, ob, oc, hg, gate_col, wa, wb, wc, layer):
    m = oa.shape[0]
    d = wa.shape[2]
    tm = min(TM_PROJ, m)
    tn = TN_WIDE if d % TN_WIDE == 0 else TN_PROJ
    nb = d // tn
    g0 = gate_col // tn
    o_spec = lambda a: pl.BlockSpec((tm, a.shape[1]), lambda i, j: (i, 0))
    g_spec = lambda br: pl.BlockSpec((tm, tn), lambda i, j: (i, g0 + br * nb + j))
    w_spec = lambda w: pl.BlockSpec((None, w.shape[1], tn), lambda i, j: (layer, 0, j))
    return pl.pallas_call(
        _merge_kernel,
        grid=(m // tm, nb),
        in_specs=[o_spec(oa), o_spec(ob), o_spec(oc), g_spec(0), g_spec(1), g_spec(2),
                  w_spec(wa), w_spec(wb), w_spec(wc)],
        out_specs=pl.BlockSpec((tm, tn), lambda i, j: (i, j)),
        out_shape=jax.ShapeDtypeStruct((m, d), BF16),
        compiler_params=_cparams("parallel", "parallel"),
        name="gated_merge",
    )(oa, ob, oc, hg, hg, hg, wa, wb, wc)


def _resid_proj_kernel(x_ref, y_ref, w_ref, o_ref):
    for c in range(0, o_ref.shape[1], PROJ_CHUNK):
        cs = slice(c, c + PROJ_CHUNK)
        o_ref[:, cs] = x_ref[:, cs] + _dot(y_ref[...], w_ref[:, cs])


def _resid_proj(x2, y, w, layer, in_place):
    m, d = x2.shape
    tm = min(TM_PROJ, m)
    tn = TN_WIDE if d % TN_WIDE == 0 else TN_PROJ
    return pl.pallas_call(
        _resid_proj_kernel,
        grid=(m // tm, d // tn),
        in_specs=[
            pl.BlockSpec((tm, tn), lambda i, j: (i, j)),
            pl.BlockSpec((tm, y.shape[1]), lambda i, j: (i, 0)),
            pl.BlockSpec((None, y.shape[1], tn), lambda i, j: (layer, 0, j)),
        ],
        out_specs=pl.BlockSpec((tm, tn), lambda i, j: (i, j)),
        out_shape=jax.ShapeDtypeStruct((m, d), F32),
        input_output_aliases={0: 0} if in_place else {},
        compiler_params=_cparams("parallel", "parallel"),
        name="resid_proj",
    )(x2, y, w)


def _ffn_kernel(x_ref, g_ref, wg_ref, wu_ref, wd_ref, o_ref, xn_ref):
    @pl.when(pl.program_id(1) == 0)
    def _():
        x = x_ref[...]
        xn_ref[...] = _rms_rows(x, g_ref[...]).astype(BF16)
        o_ref[...] = x

    xn = xn_ref[...]
    acc = None
    for c in range(0, wg_ref.shape[1], PROJ_CHUNK):
        a = _dot(xn, wg_ref[:, c:c + PROJ_CHUNK])
        u = _dot(xn, wu_ref[:, c:c + PROJ_CHUNK])
        part = _dot((_silu(a) * u).astype(BF16), wd_ref[c:c + PROJ_CHUNK, :])
        acc = part if acc is None else acc + part
    o_ref[...] += acc


def _ffn(x2, g, wg, wu, wd, layer):
    m, d = x2.shape
    f = wg.shape[2]
    tm = min(TM_FFN, m)
    tf = min(TF_FFN, f)
    return pl.pallas_call(
        _ffn_kernel,
        grid=(m // tm, f // tf),
        in_specs=[
            pl.BlockSpec((tm, d), lambda i, j: (i, 0)),
            pl.BlockSpec((1, d), lambda i, j: (0, 0)),
            pl.BlockSpec((None, d, tf), lambda i, j: (layer, 0, j)),
            pl.BlockSpec((None, d, tf), lambda i, j: (layer, 0, j)),
            pl.BlockSpec((None, tf, d), lambda i, j: (layer, j, 0)),
        ],
        out_specs=pl.BlockSpec((tm, d), lambda i, j: (i, 0)),
        out_shape=jax.ShapeDtypeStruct((m, d), F32),
        scratch_shapes=[pltpu.VMEM((tm, d), BF16)],
        input_output_aliases={0: 0},
        compiler_params=_cparams("parallel", "arbitrary"),
        name="ffn",
    )(x2, g, wg, wu, wd)


def _row(v):
    return v.astype(F32).reshape(1, -1)


def kernel(x, norm_mix, w_in, b_forget, q_norm_diff, k_norm_diff, lambda_q1, lambda_k1, lambda_q2, lambda_k2, sub_norm_diff, q_norm_fox, k_norm_fox, w_branch_sb, w_branch_diff, w_branch_fox, w_gate, w_out, norm_ffn, w_ff_gate, w_ff_up, w_ff_down):
    b, s, d = x.shape
    depth = w_in.shape[0]
    d_main = HEAD_DIM * 3 * (H_SB + H_DIFF + H_FOX)
    slopes = 2.0 ** (-ALIBI_MAX * jnp.arange(1, H_DIFF + 1, dtype=F32) / H_DIFF)
    slopes = jnp.broadcast_to(slopes[:, None, None], (H_DIFF, 1, LANES))

    w_in_b, w_gate_b, w_out_b = w_in.astype(BF16), w_gate.astype(BF16), w_out.astype(BF16)
    w_f_b = jnp.pad(w_in[:, :, d_main:], ((0, 0), (0, 0), (0, LANES - H_FOX))).astype(BF16)
    w_sb_b, w_diff_b, w_fox_b = w_branch_sb.astype(BF16), w_branch_diff.astype(BF16), w_branch_fox.astype(BF16)
    w_ffg_b, w_ffu_b, w_ffd_b = w_ff_gate.astype(BF16), w_ff_up.astype(BF16), w_ff_down.astype(BF16)

    def gain_row(l):
        c_head, c_half = HEAD_DIM ** -0.5 * LOG2E, DIFF_DIM ** -0.5 * LOG2E
        ones = lambda n: jnp.ones((n * HEAD_DIM,), F32)
        return jnp.concatenate([
            c_head * ones(H_SB), ones(2 * H_SB),
            c_half * jnp.tile(q_norm_diff[l].astype(F32), 2 * H_DIFF), jnp.tile(k_norm_diff[l].astype(F32), 2 * H_DIFF),
            ones(H_DIFF),
            c_head * jnp.tile(q_norm_fox[l].astype(F32), H_FOX), jnp.tile(k_norm_fox[l].astype(F32), H_FOX),
            ones(H_FOX)])[None, :]

    x2 = x.reshape(b * s, d)
    for l in range(depth):
        lambda_init = 0.8 - 0.6 * float(np.exp(-0.3 * l))
        b_pad = jnp.pad(_row(b_forget[l]), ((0, 0), (0, LANES - H_FOX)))

        hg, hf = _in_proj(x2, _row(norm_mix[l]), gain_row(l), w_in_b, w_gate_b, w_f_b, l)

        cum, cum_col = _forget_cum(hf.reshape(b, s, LANES), b_pad)
        h3 = hg.reshape(b, s, hg.shape[1])

        o_a = _stick_breaking(h3, b, s)
        lam_params = jnp.stack([lambda_q1[l], lambda_k1[l], lambda_q2[l], lambda_k2[l]]).astype(F32)
        o_b = _diff_attention(h3, b, s,
                              _row(jnp.tile(q_norm_diff[l], 2)), _row(jnp.tile(k_norm_diff[l], 2)),
                              _row(sub_norm_diff[l]), lam_params, slopes, lambda_init)
        o_c = _fox_attention(h3, b, s, _row(q_norm_fox[l]), _row(k_norm_fox[l]),
                             cum.reshape(b, SUBLANES, 1, s), cum_col)

        y = _gated_merge(o_a.reshape(b * s, -1), o_b.reshape(b * s, -1), o_c.reshape(b * s, -1), hg, d_main,
                         w_sb_b, w_diff_b, w_fox_b, l)
        x2 = _resid_proj(x2, y, w_out_b, l, in_place=l > 0)
        x2 = _ffn(x2, _row(norm_ffn[l]), w_ffg_b, w_ffu_b, w_ffd_b, l)
    return x2.reshape(b, s, d)
```

```python
import functools

import jax
import jax.numpy as jnp
import numpy as np
from jax import lax
from jax.experimental import pallas as pl
from jax.experimental.pallas import tpu as pltpu

F32 = jnp.float32
BF16 = jnp.bfloat16

HEAD_DIM = 128
DIFF_DIM = HEAD_DIM // 2
H_SB = 6
H_DIFF = 5
H_FOX = 5
CHUNK = 64
EPS = 1e-6
ALIBI_MAX = 8.0

LANES = 128
SUBLANES = 8
VMEM_LIMIT_BYTES = 56 * 1024 * 1024

TM_PROJ = 1024
TN_PROJ = 512
TN_IN_PROJ = 1536
TM_WIDE = 512
TN_WIDE = 2048
PROJ_CHUNK = 256
TM_FFN = 1024
TF_FFN = 512
TQ = 256
INTERLEAVE_SB = 4
INTERLEAVE_DIFF = 3
INTERLEAVE_FOX = 3
INTERLEAVE_BOUNDED = 2
NEG = -1e30
MAX_LOGIT_BOUND = 48.0
LOG2E = float(np.log2(np.e))


def _cparams(*sem):
    return pltpu.CompilerParams(dimension_semantics=sem, vmem_limit_bytes=VMEM_LIMIT_BYTES)


def _rms_rows(x, g):
    ms = jnp.mean(x * x, axis=-1, keepdims=True)
    return x * lax.rsqrt(ms + EPS) * g


def _sigmoid(z):
    return 0.5 * jnp.tanh(0.5 * z) + 0.5


def _silu(z):
    h = 0.5 * z
    return h * jnp.tanh(h) + h


def _interleave(tasks, width):
    tasks = iter(tasks)
    active = [t for _, t in zip(range(width), tasks)]
    while active:
        for t in list(active):
            try:
                next(t)
            except StopIteration:
                active.remove(t)
                nxt = next(tasks, None)
                if nxt is not None:
                    active.append(nxt)


def _dot_nt(a, b):
    return lax.dot_general(a, b, (((1,), (1,)), ((), ())), preferred_element_type=F32)


def _dot(a, b):
    return jnp.dot(a, b, preferred_element_type=F32)


def _head_norm(hidx):
    diff0 = 3 * H_SB
    fox0 = diff0 + 3 * H_DIFF
    if diff0 <= hidx < diff0 + 2 * H_DIFF:
        return "half"
    if fox0 <= hidx < fox0 + 2 * H_FOX:
        return "full"
    return None


def _in_proj_kernel(x_ref, g_ref, cs_ref, w_ref, wg_ref, wf_ref, o_ref, hf_ref, xn_ref, *, n_plain):
    j = pl.program_id(1)
    tn = o_ref.shape[1]
    lo_lane = lax.broadcasted_iota(jnp.int32, (1, HEAD_DIM), 1) < DIFF_DIM

    @pl.when(j == 0)
    def _():
        xn_ref[...] = _rms_rows(x_ref[...], g_ref[...]).astype(BF16)
        hf_ref[...] = _dot(xn_ref[...], wf_ref[...])

    def project(wt_ref, act, order=None):
        for c in order or range(0, tn, PROJ_CHUNK):
            cs = slice(c, c + PROJ_CHUNK)
            o_ref[:, cs] = act(_dot(xn_ref[...], wt_ref[:, cs]), c).astype(o_ref.dtype)

    def heads(tile):
        def act(a, c):
            outs = []
            for hh in range(0, PROJ_CHUNK, HEAD_DIM):
                sub, gain = a[:, hh:hh + HEAD_DIM], cs_ref[:, c + hh:c + hh + HEAD_DIM]
                norm = _head_norm((tile * tn + c + hh) // HEAD_DIM)
                if norm == "half":
                    outs.append(_halves_rms(sub, gain, lo_lane))
                elif norm == "full":
                    outs.append(_rms_rows(sub, gain))
                else:
                    outs.append(sub * gain)
            return jnp.concatenate(outs, axis=1)
        return act

    def normed(tile, c):
        return any(_head_norm((tile * tn + c + hh) // HEAD_DIM) for hh in range(0, PROJ_CHUNK, HEAD_DIM))

    for tile in range(n_plain):
        @pl.when(j == tile)
        def _(tile=tile):
            order = sorted(range(0, tn, PROJ_CHUNK), key=lambda c: not normed(tile, c))
            project(w_ref, heads(tile), order)

    @pl.when(j >= n_plain)
    def _():
        project(wg_ref, lambda a, c: _sigmoid(a))


def _in_proj(x2, g, col_scale, w_in, w_gate, w_f, layer):
    m, d = x2.shape
    d_plain, d_gate = col_scale.shape[1], w_gate.shape[2]
    tm = min(TM_PROJ, m)
    tn = TN_IN_PROJ if (d_gate % TN_IN_PROJ == 0 and d_plain % TN_IN_PROJ == 0) else TN_PROJ
    n_plain = d_plain // tn
    return pl.pallas_call(
        functools.partial(_in_proj_kernel, n_plain=n_plain),
        grid=(m // tm, (d_plain + d_gate) // tn),
        in_specs=[
            pl.BlockSpec((tm, d), lambda i, j: (i, 0)),
            pl.BlockSpec((1, d), lambda i, j: (0, 0)),
            pl.BlockSpec((1, tn), lambda i, j: (0, jnp.minimum(j, n_plain - 1))),
            pl.BlockSpec((None, d, tn), lambda i, j: (layer, 0, jnp.minimum(j, n_plain - 1))),
            pl.BlockSpec((None, d, tn), lambda i, j: (layer, 0, jnp.maximum(j - n_plain, 0))),
            pl.BlockSpec((None, d, LANES), lambda i, j: (layer, 0, 0)),
        ],
        out_specs=[pl.BlockSpec((tm, tn), lambda i, j: (i, j)),
                   pl.BlockSpec((tm, LANES), lambda i, j: (i, 0))],
        out_shape=[jax.ShapeDtypeStruct((m, d_plain + d_gate), BF16), jax.ShapeDtypeStruct((m, LANES), F32)],
        scratch_shapes=[pltpu.VMEM((tm, d), BF16)],
        compiler_params=_cparams("parallel", "arbitrary"),
        name="in_proj",
    )(x2, g, col_scale, w_in, w_gate, w_f)


def _split3(x):
    p0 = x.astype(BF16)
    r0 = x - p0.astype(F32)
    p1 = r0.astype(BF16)
    return p0, p1, (r0 - p1.astype(F32)).astype(BF16)


def _forget_cum_kernel(hf_ref, b_ref, o_ref, c_ref, *, blk):
    s = hf_ref.shape[1]
    z = hf_ref[0] + b_ref[...]
    lf = jnp.minimum(z, 0.0) - jnp.log(1.0 + jnp.exp(-jnp.abs(z)))
    lft = lf.T[:SUBLANES]
    row = lax.broadcasted_iota(jnp.int32, (blk, blk), 0)
    col = lax.broadcasted_iota(jnp.int32, (blk, blk), 1)
    upper = (row <= col).astype(BF16)
    lower = (row >= col).astype(BF16)
    carry = jnp.zeros((SUBLANES, 1), F32)
    carry_c = jnp.zeros((1, LANES), F32)
    for i in range(s // blk):
        p0, p1, p2 = _split3(lft[:, i * blk:(i + 1) * blk])
        cs = _dot(p0, upper) + _dot(p1, upper) + _dot(p2, upper) + carry
        o_ref[0, :, i * blk:(i + 1) * blk] = cs
        carry = cs[:, blk - 1:blk]
        p0, p1, p2 = _split3(lf[i * blk:(i + 1) * blk, :])
        cc = _dot(lower, p0) + _dot(lower, p1) + _dot(lower, p2) + carry_c
        c_ref[0, i * blk:(i + 1) * blk, :] = cc
        carry_c = cc[blk - 1:blk, :]


def _forget_cum(hf, b_pad):
    b, s, _ = hf.shape
    blk = min(256, s)
    return pl.pallas_call(
        functools.partial(_forget_cum_kernel, blk=blk),
        grid=(b,),
        in_specs=[
            pl.BlockSpec((1, s, LANES), lambda i: (i, 0, 0)),
            pl.BlockSpec((1, LANES), lambda i: (0, 0)),
        ],
        out_specs=[pl.BlockSpec((1, SUBLANES, s), lambda i: (i, 0, 0)),
                   pl.BlockSpec((1, s, LANES), lambda i: (i, 0, 0))],
        out_shape=[jax.ShapeDtypeStruct((b, SUBLANES, s), F32), jax.ShapeDtypeStruct((b, s, LANES), F32)],
        compiler_params=_cparams("parallel"),
        name="forget_cum",
    )(hf, b_pad)


def _sb_kernel(q_ref, k_ref, v_ref, o_ref, *, tq):
    s = q_ref.shape[1]
    row = lax.broadcasted_iota(jnp.int32, (tq, tq), 0)
    col = lax.broadcasted_iota(jnp.int32, (tq, tq), 1)
    neg_tri = -((row >= col).astype(BF16))
    before = col < row

    def q_block(i):
        q0 = i * tq
        q = q_ref[0, q0:q0 + tq, :]
        zs, sufs = [], []
        for j in range(i + 1):
            z = _dot_nt(q, k_ref[0, j * tq:(j + 1) * tq, :])
            sp = jnp.maximum(z, 0.0) + jnp.log2(1.0 + jnp.exp2(-jnp.abs(z)))
            if j == i:
                sp = jnp.where(before, sp, 0.0)
            zs.append(z)
            yield
            sufs.append(_dot(sp.astype(BF16), neg_tri))
            yield
        carry = None
        acc = None
        for j in reversed(range(i + 1)):
            arg = zs[j] + sufs[j]
            if carry is not None:
                arg = arg + carry
            if j == i:
                arg = jnp.where(before, arg, NEG)
            pv = _dot(jnp.exp2(arg).astype(BF16), v_ref[0, j * tq:(j + 1) * tq, :])
            acc = pv if acc is None else acc + pv
            tot = sufs[j][:, 0:1]
            carry = tot if carry is None else carry + tot
            yield
        o_ref[0, q0:q0 + tq, :] = acc.astype(o_ref.dtype)

    _interleave([q_block(i) for i in range(s // tq)], INTERLEAVE_SB)


def _stick_breaking(h3, b, s):
    tq = min(TQ, s)
    blk = lambda off: pl.BlockSpec((1, s, HEAD_DIM), lambda bi, hi: (bi, 0, off + hi))
    return pl.pallas_call(
        functools.partial(_sb_kernel, tq=tq),
        grid=(b, H_SB),
        in_specs=[blk(0), blk(H_SB), blk(2 * H_SB)],
        out_specs=pl.BlockSpec((1, s, HEAD_DIM), lambda bi, hi: (bi, 0, hi)),
        out_shape=jax.ShapeDtypeStruct((b, s, H_SB * HEAD_DIM), BF16),
        compiler_params=_cparams("parallel", "parallel"),
        name="stick_breaking",
    )(h3, h3, h3)


def _halves_rms(x, g2, lo_lane):
    x2 = x * x
    s_lo = jnp.sum(jnp.where(lo_lane, x2, 0.0), axis=-1, keepdims=True)
    s_hi = jnp.sum(jnp.where(lo_lane, 0.0, x2), axis=-1, keepdims=True)
    inv = jnp.where(lo_lane, lax.rsqrt(s_lo / DIFF_DIM + EPS), lax.rsqrt(s_hi / DIFF_DIM + EPS))
    return x * inv * g2


def _diff_kernel(q_ref, k_ref, v_ref, gs_ref, lam_ref, slope_ref, mb_ref, o_ref,
                 vx_ref, *bias_ref, tq, lambda_init, bounded):
    s = q_ref.shape[1]
    lo_lane = lax.broadcasted_iota(jnp.int32, (1, HEAD_DIM), 1) < DIFF_DIM
    row = lax.broadcasted_iota(jnp.int32, (tq, tq), 0)
    col = lax.broadcasted_iota(jnp.int32, (tq, tq), 1)
    visible = (col // CHUNK) <= (row // CHUNK)
    slope = slope_ref[0][:, 0:1] * LOG2E
    diag_bias = (row - jnp.abs(row - col)).astype(F32) * slope
    key_pos = lax.broadcasted_iota(jnp.int32, (1, s), 1).astype(F32) * slope
    lp = lam_ref[...]
    lam = (jnp.exp(jnp.sum(lp[0:1] * lp[1:2], axis=-1, keepdims=True))
           - jnp.exp(jnp.sum(lp[2:3] * lp[3:4], axis=-1, keepdims=True)) + lambda_init)

    vx_ref[:, :HEAD_DIM] = v_ref[0]
    vx_ref[:, HEAD_DIM:] = jnp.ones((s, HEAD_DIM), BF16)

    def rowmax(x):
        return jnp.max(x, axis=-1, keepdims=True)

    def stacked_q(q0):
        q = q_ref[0, q0:q0 + tq, :]
        zero = jnp.zeros_like(q)
        return jnp.concatenate([jnp.where(lo_lane, q, zero), jnp.where(lo_lane, zero, q)], axis=0)

    def q_block(i):
        q0 = i * tq
        q12 = stacked_q(q0)
        bo = key_pos[:, :q0 + tq] - slope * q0
        s1, s2 = [], []
        for j in range(i + 1):
            z = _dot_nt(q12, k_ref[0, j * tq:(j + 1) * tq, :])
            if j == i:
                s1.append(jnp.where(visible, z[:tq] + diag_bias, NEG))
                s2.append(jnp.where(visible, z[tq:] + diag_bias, NEG))
            else:
                s1.append(z[:tq] + bo[:, j * tq:(j + 1) * tq])
                s2.append(z[tq:] + bo[:, j * tq:(j + 1) * tq])
            yield
        m1 = rowmax(functools.reduce(jnp.maximum, s1))
        m2 = rowmax(functools.reduce(jnp.maximum, s2))
        acc = None
        for j in range(i + 1):
            p12 = jnp.concatenate([jnp.exp2(s1[j] - m1), jnp.exp2(s2[j] - m2)], axis=0).astype(BF16)
            pv = _dot(p12, vx_ref[j * tq:(j + 1) * tq, :])
            acc = pv if acc is None else acc + pv
            yield
        finish(q0, acc)

    def finish(q0, acc):
        o = (acc[:tq, :HEAD_DIM] / acc[:tq, HEAD_DIM:]
             - acc[tq:, :HEAD_DIM] * (lam / acc[tq:, HEAD_DIM:]))
        o = _rms_rows(o, gs_ref[...]) * (1.0 - lambda_init)
        o_ref[0, q0:q0 + tq, :] = o.astype(o_ref.dtype)

    def q_block_bounded(i):
        q0 = i * tq
        q12 = stacked_q(q0)
        acc = None
        for j in range(i + 1):
            z = _dot_nt(q12, k_ref[0, j * tq:(j + 1) * tq, :])
            bias = bias_ref[0][i - j]
            p12 = jnp.exp2(jnp.concatenate([z[:tq] + bias, z[tq:] + bias], axis=0)).astype(BF16)
            pv = _dot(p12, vx_ref[j * tq:(j + 1) * tq, :])
            acc = pv if acc is None else acc + pv
            yield
        finish(q0, acc)

    if bounded:
        dist = (row - col).astype(F32)
        mb = mb_ref[...]
        bias_ref[0][0] = jnp.where(visible, -slope * jnp.abs(dist) - mb, NEG)
        for dlt in range(1, s // tq):
            bias_ref[0][dlt] = -slope * (dist + float(dlt * tq)) - mb
        _interleave([q_block_bounded(i) for i in range(s // tq)], INTERLEAVE_BOUNDED)
    else:
        _interleave([q_block(i) for i in range(s // tq)], INTERLEAVE_DIFF)


def _logit_bound(gq, gk, dim):
    return (jnp.max(jnp.abs(gq)) * jnp.max(jnp.abs(gk)) * (dim ** 0.5 * LOG2E * 1.02)).astype(F32).reshape(1, 1)


def _diff_attention(h3, b, s, gq, gk, gs, lam_params, slopes, lambda_init):
    tq = min(TQ, s)
    off = 3 * H_SB
    blk = lambda o: pl.BlockSpec((1, s, HEAD_DIM), lambda bi, hi: (bi, 0, o + hi))
    vec = pl.BlockSpec((1, HEAD_DIM), lambda bi, hi: (0, 0))
    one = pl.BlockSpec((1, 1), lambda bi, hi: (0, 0))
    mb = _logit_bound(gq, gk, DIFF_DIM)

    def call(bounded):
        bias = [pltpu.VMEM((s // tq, tq, tq), F32)] if bounded else []
        return pl.pallas_call(
            functools.partial(_diff_kernel, tq=tq, lambda_init=lambda_init, bounded=bounded),
            grid=(b, H_DIFF),
            in_specs=[blk(off), blk(off + H_DIFF), blk(off + 2 * H_DIFF), vec,
                      pl.BlockSpec((4, DIFF_DIM), lambda bi, hi: (0, 0)),
                      pl.BlockSpec((1, 1, LANES), lambda bi, hi: (hi, 0, 0)), one],
            out_specs=pl.BlockSpec((1, s, HEAD_DIM), lambda bi, hi: (bi, 0, hi)),
            out_shape=jax.ShapeDtypeStruct((b, s, H_DIFF * HEAD_DIM), BF16),
            scratch_shapes=[pltpu.VMEM((s, 2 * HEAD_DIM), BF16)] + bias,
            compiler_params=_cparams("parallel", "parallel"),
            name="diff_attention_bounded" if bounded else "diff_attention",
        )

    return lax.cond(mb[0, 0] < MAX_LOGIT_BOUND, call(True), call(False),
                    h3, h3, h3, gs, lam_params, slopes, mb)


def _fox_kernel(q_ref, k_ref, v_ref, cum_ref, cumc_ref, mb_ref, o_ref, *vx_ref, tq, bounded):
    s = q_ref.shape[1]
    row = lax.broadcasted_iota(jnp.int32, (tq, tq), 0)
    col = lax.broadcasted_iota(jnp.int32, (tq, tq), 1)
    causal = col <= row

    def q_block(i):
        q0 = i * tq
        q = q_ref[0, q0:q0 + tq, :]
        cum = cum_ref[0, 0, :, 0:q0 + tq]
        bias = (cum[:, q0:q0 + 1] - cum) * LOG2E
        ss = []
        for j in range(i + 1):
            sj = _dot_nt(q, k_ref[0, j * tq:(j + 1) * tq, :]) + bias[:, j * tq:(j + 1) * tq]
            ss.append(jnp.where(causal, sj, NEG) if j == i else sj)
            yield
        m = jnp.max(functools.reduce(jnp.maximum, ss), axis=-1, keepdims=True)
        ps = []
        for sj in ss:
            ps.append(jnp.exp2(sj - m))
            yield
        l = jnp.sum(functools.reduce(jnp.add, ps), axis=-1, keepdims=True)
        acc = None
        for j, pj in enumerate(ps):
            pv = _dot(pj.astype(BF16), v_ref[0, j * tq:(j + 1) * tq, :])
            acc = pv if acc is None else acc + pv
            yield
        o_ref[0, q0:q0 + tq, :] = (acc / l).astype(o_ref.dtype)

    def q_block_bounded(i):
        q0 = i * tq
        q = q_ref[0, q0:q0 + tq, :]
        head = lax.broadcasted_iota(jnp.int32, (1, LANES), 1) == pl.program_id(1)
        f_t = jnp.sum(jnp.where(head, cumc_ref[0, q0:q0 + tq, :], 0.0), axis=-1, keepdims=True)
        row_term = f_t * LOG2E - mb_ref[...]
        col_term = cum_ref[0, 0, :, 0:q0 + tq] * LOG2E
        acc = None
        for j in range(i + 1):
            sj = _dot_nt(q, k_ref[0, j * tq:(j + 1) * tq, :]) + row_term - col_term[:, j * tq:(j + 1) * tq]
            if j == i:
                sj = jnp.where(causal, sj, NEG)
            pv = _dot(jnp.exp2(sj).astype(BF16), vx_ref[0][j * tq:(j + 1) * tq, :])
            acc = pv if acc is None else acc + pv
            yield
        o_ref[0, q0:q0 + tq, :] = (acc[:, :HEAD_DIM] / acc[:, HEAD_DIM:]).astype(o_ref.dtype)

    if bounded:
        vx_ref[0][:, :HEAD_DIM] = v_ref[0]
        vx_ref[0][:, HEAD_DIM:] = jnp.ones((s, HEAD_DIM), BF16)
        _interleave([q_block_bounded(i) for i in range(s // tq)], INTERLEAVE_BOUNDED)
    else:
        _interleave([q_block(i) for i in range(s // tq)], INTERLEAVE_FOX)


def _fox_attention(h3, b, s, gq, gk, cum4, cum_col):
    tq = min(TQ, s)
    off = 3 * H_SB + 3 * H_DIFF
    blk = lambda o: pl.BlockSpec((1, s, HEAD_DIM), lambda bi, hi: (bi, 0, o + hi))
    mb = _logit_bound(gq, gk, HEAD_DIM)

    def call(bounded):
        vx = [pltpu.VMEM((s, 2 * HEAD_DIM), BF16)] if bounded else []
        return pl.pallas_call(
            functools.partial(_fox_kernel, tq=tq, bounded=bounded),
            grid=(b, H_FOX),
            in_specs=[blk(off), blk(off + H_FOX), blk(off + 2 * H_FOX),
                      pl.BlockSpec((1, 1, 1, s), lambda bi, hi: (bi, hi, 0, 0)),
                      pl.BlockSpec((1, s, LANES), lambda bi, hi: (bi, 0, 0)),
                      pl.BlockSpec((1, 1), lambda bi, hi: (0, 0))],
            out_specs=pl.BlockSpec((1, s, HEAD_DIM), lambda bi, hi: (bi, 0, hi)),
            out_shape=jax.ShapeDtypeStruct((b, s, H_FOX * HEAD_DIM), BF16),
            scratch_shapes=vx,
            compiler_params=_cparams("parallel", "parallel"),
            name="fox_attention_bounded" if bounded else "fox_attention",
        )

    return lax.cond(mb[0, 0] < MAX_LOGIT_BOUND, call(True), call(False), h3, h3, h3, cum4, cum_col, mb)


def _merge_kernel(oa_ref, ob_ref, oc_ref, ga_ref, gb_ref, gc_ref, wa_ref, wb_ref, wc_ref, y_ref):
    for c in range(0, y_ref.shape[1], PROJ_CHUNK):
        cs = slice(c, c + PROJ_CHUNK)
        y = ga_ref[:, cs].astype(F32) * _dot(oa_ref[...], wa_ref[:, cs])
        y += gb_ref[:, cs].astype(F32) * _dot(ob_ref[...], wb_ref[:, cs])
        y += gc_ref[:, cs].astype(F32) * _dot(oc_ref[...], wc_ref[:, cs])
        y_ref[:, cs] = y.astype(y_ref.dtype)


def _gated_merge(oa, ob, oc, hg, gate_col, wa, wb, wc, layer):
    m = oa.shape[0]
    d = wa.shape[2]
    tm = min(TM_WIDE, m)
    tn = TN_WIDE if d % TN_WIDE == 0 else TN_PROJ
    nb = d // tn
    g0 = gate_col // tn
    o_spec = lambda a: pl.BlockSpec((tm, a.shape[1]), lambda i, j: (i, 0))
    g_spec = lambda br: pl.BlockSpec((tm, tn), lambda i, j: (i, g0 + br * nb + j))
    w_spec = lambda w: pl.BlockSpec((None, w.shape[1], tn), lambda i, j: (layer, 0, j))
    return pl.pallas_call(
        _merge_kernel,
        grid=(m // tm, nb),
        in_specs=[o_spec(oa), o_spec(ob), o_spec(oc), g_spec(0), g_spec(1), g_spec(2),
                  w_spec(wa), w_spec(wb), w_spec(wc)],
        out_specs=pl.BlockSpec((tm, tn), lambda i, j: (i, j)),
        out_shape=jax.ShapeDtypeStruct((m, d), BF16),
        compiler_params=_cparams("parallel", "parallel"),
        name="gated_merge",
    )(oa, ob, oc, hg, hg, hg, wa, wb, wc)


def _resid_proj_kernel(x_ref, y_ref, w_ref, o_ref):
    for c in range(0, o_ref.shape[1], PROJ_CHUNK):
        cs = slice(c, c + PROJ_CHUNK)
        o_ref[:, cs] = x_ref[:, cs] + _dot(y_ref[...], w_ref[:, cs])


def _resid_proj(x2, y, w, layer, in_place):
    m, d = x2.shape
    tm = min(TM_WIDE, m)
    tn = TN_WIDE if d % TN_WIDE == 0 else TN_PROJ
    return pl.pallas_call(
        _resid_proj_kernel,
        grid=(m // tm, d // tn),
        in_specs=[
            pl.BlockSpec((tm, tn), lambda i, j: (i, j)),
            pl.BlockSpec((tm, y.shape[1]), lambda i, j: (i, 0)),
            pl.BlockSpec((None, y.shape[1], tn), lambda i, j: (layer, 0, j)),
        ],
        out_specs=pl.BlockSpec((tm, tn), lambda i, j: (i, j)),
        out_shape=jax.ShapeDtypeStruct((m, d), F32),
        input_output_aliases={0: 0} if in_place else {},
        compiler_params=_cparams("parallel", "parallel"),
        name="resid_proj",
    )(x2, y, w)


def _ffn_kernel(x_ref, g_ref, wg_ref, wu_ref, wd_ref, o_ref, xn_ref):
    @pl.when(pl.program_id(1) == 0)
    def _():
        x = x_ref[...]
        xn_ref[...] = _rms_rows(x, g_ref[...]).astype(BF16)
        o_ref[...] = x

    xn = xn_ref[...]
    acc = None
    for c in range(0, wg_ref.shape[1], PROJ_CHUNK):
        a = _dot(xn, wg_ref[:, c:c + PROJ_CHUNK])
        u = _dot(xn, wu_ref[:, c:c + PROJ_CHUNK])
        part = _dot((_silu(a) * u).astype(BF16), wd_ref[c:c + PROJ_CHUNK, :])
        acc = part if acc is None else acc + part
    o_ref[...] += acc


def _ffn(x2, g, wg, wu, wd, layer):
    m, d = x2.shape
    f = wg.shape[2]
    tm = min(TM_FFN, m)
    tf = min(TF_FFN, f)
    return pl.pallas_call(
        _ffn_kernel,
        grid=(m // tm, f // tf),
        in_specs=[
            pl.BlockSpec((tm, d), lambda i, j: (i, 0)),
            pl.BlockSpec((1, d), lambda i, j: (0, 0)),
            pl.BlockSpec((None, d, tf), lambda i, j: (layer, 0, j)),
            pl.BlockSpec((None, d, tf), lambda i, j: (layer, 0, j)),
            pl.BlockSpec((None, tf, d), lambda i, j: (layer, j, 0)),
        ],
        out_specs=pl.BlockSpec((tm, d), lambda i, j: (i, 0)),
        out_shape=jax.ShapeDtypeStruct((m, d), F32),
        scratch_shapes=[pltpu.VMEM((tm, d), BF16)],
        input_output_aliases={0: 0},
        compiler_params=_cparams("parallel", "arbitrary"),
        name="ffn",
    )(x2, g, wg, wu, wd)


def _row(v):
    return v.astype(F32).reshape(1, -1)


def kernel(x, norm_mix, w_in, b_forget, q_norm_diff, k_norm_diff, lambda_q1, lambda_k1, lambda_q2, lambda_k2, sub_norm_diff, q_norm_fox, k_norm_fox, w_branch_sb, w_branch_diff, w_branch_fox, w_gate, w_out, norm_ffn, w_ff_gate, w_ff_up, w_ff_down):
    b, s, d = x.shape
    depth = w_in.shape[0]
    d_main = HEAD_DIM * 3 * (H_SB + H_DIFF + H_FOX)
    slopes = 2.0 ** (-ALIBI_MAX * jnp.arange(1, H_DIFF + 1, dtype=F32) / H_DIFF)
    slopes = jnp.broadcast_to(slopes[:, None, None], (H_DIFF, 1, LANES))

    w_in_b, w_gate_b, w_out_b = w_in.astype(BF16), w_gate.astype(BF16), w_out.astype(BF16)
    w_f_b = jnp.pad(w_in[:, :, d_main:], ((0, 0), (0, 0), (0, LANES - H_FOX))).astype(BF16)
    w_sb_b, w_diff_b, w_fox_b = w_branch_sb.astype(BF16), w_branch_diff.astype(BF16), w_branch_fox.astype(BF16)
    w_ffg_b, w_ffu_b, w_ffd_b = w_ff_gate.astype(BF16), w_ff_up.astype(BF16), w_ff_down.astype(BF16)

    def gain_row(l):
        c_head, c_half = HEAD_DIM ** -0.5 * LOG2E, DIFF_DIM ** -0.5 * LOG2E
        ones = lambda n: jnp.ones((n * HEAD_DIM,), F32)
        return jnp.concatenate([
            c_head * ones(H_SB), ones(2 * H_SB),
            c_half * jnp.tile(q_norm_diff[l].astype(F32), 2 * H_DIFF), jnp.tile(k_norm_diff[l].astype(F32), 2 * H_DIFF),
            ones(H_DIFF),
            c_head * jnp.tile(q_norm_fox[l].astype(F32), H_FOX), jnp.tile(k_norm_fox[l].astype(F32), H_FOX),
            ones(H_FOX)])[None, :]

    x2 = x.reshape(b * s, d)
    for l in range(depth):
        lambda_init = 0.8 - 0.6 * float(np.exp(-0.3 * l))
        b_pad = jnp.pad(_row(b_forget[l]), ((0, 0), (0, LANES - H_FOX)))

        hg, hf = _in_proj(x2, _row(norm_mix[l]), gain_row(l), w_in_b, w_gate_b, w_f_b, l)

        cum, cum_col = _forget_cum(hf.reshape(b, s, LANES), b_pad)
        h3 = hg.reshape(b, s, hg.shape[1])

        o_a = _stick_breaking(h3, b, s)
        lam_params = jnp.stack([lambda_q1[l], lambda_k1[l], lambda_q2[l], lambda_k2[l]]).astype(F32)
        o_b = _diff_attention(h3, b, s,
                              _row(jnp.tile(q_norm_diff[l], 2)), _row(jnp.tile(k_norm_diff[l], 2)),
                              _row(sub_norm_diff[l]), lam_params, slopes, lambda_init)
        o_c = _fox_attention(h3, b, s, _row(q_norm_fox[l]), _row(k_norm_fox[l]),
                             cum.reshape(b, SUBLANES, 1, s), cum_col)

        y = _gated_merge(o_a.reshape(b * s, -1), o_b.reshape(b * s, -1), o_c.reshape(b * s, -1), hg, d_main,
                         w_sb_b, w_diff_b, w_fox_b, l)
        x2 = _resid_proj(x2, y, w_out_b, l, in_place=l > 0)
        x2 = _ffn(x2, _row(norm_ffn[l]), w_ffg_b, w_ffu_b, w_ffd_b, l)
    return x2.reshape(b, s, d)
```
